```python
import jax, jax.numpy as jnp
from jax import lax
import numpy as np

D_MODEL = 1024
BATCH = 8
SEQ = 8192
DEPTH = 1
DEC_BATCH = 2
DEC_SEQ = 8192
PAST_LEN = 128

GRID_W = 64
HEAD_DIM = 64
N_HEADS_A = 8
N_KV_A = 2
N_HEADS_B = 8
D_MIX = (N_HEADS_A + N_HEADS_B) * HEAD_DIM
Q_BLOCK = 128
ROPE_THETA = 10000.0
NB_ROWS_MAX = 8
NB_COLS = 16
N_MEM = 256
MEM_HEADS = 4
MEM_HEAD_DIM = D_MODEL // MEM_HEADS
N_EXPERTS = 16
EC_FACTOR = 2
D_FF = 2 * D_MODEL
EPS = 1e-6

QA_W = N_HEADS_A * HEAD_DIM
KA_W = N_KV_A * HEAD_DIM
VA_W = N_KV_A * HEAD_DIM
QB_W = N_HEADS_B * HEAD_DIM
KB_W = N_HEADS_B * HEAD_DIM
VB_W = N_HEADS_B * HEAD_DIM
D_IN = QA_W + KA_W + VA_W + QB_W + KB_W + VB_W
SPLITS = [QA_W, QA_W + KA_W, QA_W + KA_W + VA_W, QA_W + KA_W + VA_W + QB_W,
          QA_W + KA_W + VA_W + QB_W + KB_W]

kernel_name = 'hybrid_gqa_natten_ec_encoder'


def rmsnorm(x, g):
    x32 = x.astype(jnp.float32)
    y = x32 * lax.rsqrt(jnp.mean(x32 * x32, axis=-1, keepdims=True) + EPS)
    return (y * g.astype(jnp.float32)).astype(x.dtype)


def axial_rope(seq_len, dtype):
    half = HEAD_DIM // 2
    inv = ROPE_THETA ** (-jnp.arange(0, half, 2, dtype=jnp.float32) / half)
    t = jnp.arange(seq_len)
    row = (t // GRID_W).astype(jnp.float32)
    col = (t % GRID_W).astype(jnp.float32)
    ar = row[:, None] * inv[None]
    ac = col[:, None] * inv[None]
    ang = jnp.concatenate([ar, ar, ac, ac], axis=-1)
    return jnp.cos(ang).astype(dtype), jnp.sin(ang).astype(dtype)


def apply_rope(x, cos, sin):
    x1, x2, x3, x4 = jnp.split(x, 4, axis=-1)
    rot = jnp.concatenate([-x2, x1, -x4, x3], axis=-1)
    return x * cos + rot * sin


def global_gqa(q, k, v, q_gain, k_gain):
    B, S = q.shape[0], q.shape[1]
    q = rmsnorm(q, q_gain)
    k = rmsnorm(k, k_gain)
    cos, sin = axial_rope(S, q.dtype)
    q = apply_rope(q, cos[:, None], sin[:, None])
    k = apply_rope(k, cos[:, None], sin[:, None])
    G = N_HEADS_A // N_KV_A
    nb = S // Q_BLOCK
    qb = q.reshape(B, nb, Q_BLOCK, N_KV_A, G, HEAD_DIM).transpose(1, 0, 3, 4, 2, 5)
    kt = k.transpose(0, 2, 1, 3)
    vt = v.transpose(0, 2, 1, 3)
    scale = HEAD_DIM ** -0.5

    def block(qblk):
        s = jnp.einsum('bkgqd,bksd->bkgqs', qblk, kt).astype(jnp.float32) * scale
        p = jax.nn.softmax(s, axis=-1).astype(vt.dtype)
        return jnp.einsum('bkgqs,bksd->bkgqd', p, vt)

    o = lax.map(block, qb)
    return o.transpose(1, 0, 4, 2, 3, 5).reshape(B, S, N_HEADS_A * HEAD_DIM)


def neighbourhood_attn(q, k, v, rpb):
    B, S, H, d = q.shape
    rows = S // GRID_W
    kr = min(NB_ROWS_MAX, rows)
    kc = NB_COLS
    qg = q.reshape(B, rows, GRID_W, H, d).transpose(1, 0, 3, 2, 4)
    kg = k.reshape(B, rows, GRID_W, H, d).transpose(0, 3, 1, 2, 4)
    vg = v.reshape(B, rows, GRID_W, H, d).transpose(0, 3, 1, 2, 4)
    qcol = jnp.arange(GRID_W)
    cstart = jnp.clip(qcol - kc // 2, 0, GRID_W - kc)
    cidx = cstart[:, None] + jnp.arange(kc)[None]
    dc = cidx - qcol[:, None] + (NB_COLS - 1)
    scale = d ** -0.5

    def row_block(args):
        r, qr = args
        rs = jnp.clip(r - kr // 2, 0, rows - kr)
        ks = lax.dynamic_slice_in_dim(kg, rs, kr, axis=2)
        vs = lax.dynamic_slice_in_dim(vg, rs, kr, axis=2)
        kn = ks[:, :, :, cidx]
        vn = vs[:, :, :, cidx]
        dr = rs + jnp.arange(kr) - r + (NB_ROWS_MAX - 1)
        bias = rpb[:, dr[:, None, None], dc[None]]
        bias = bias.transpose(0, 2, 1, 3).astype(jnp.float32)
        s = jnp.einsum('bhwd,bhrwcd->bhwrc', qr, kn).astype(jnp.float32) * scale + bias[None]
        p = jax.nn.softmax(s.reshape(B, H, GRID_W, kr * kc), axis=-1)
        p = p.reshape(B, H, GRID_W, kr, kc).astype(vn.dtype)
        return jnp.einsum('bhwrc,bhrwcd->bhwd', p, vn)

    o = lax.map(row_block, (jnp.arange(rows), qg))
    return o.transpose(1, 0, 3, 2, 4).reshape(B, S, H * d)


def memory_xattn(h, mem, mem_gain, w_mq, w_mkv, w_mo):
    B, S, D = h.shape
    M = mem.shape[1]
    m = rmsnorm(mem, mem_gain)
    q = (h @ w_mq).reshape(B, S, MEM_HEADS, MEM_HEAD_DIM)
    kv = (m @ w_mkv).reshape(B, M, 2, MEM_HEADS, MEM_HEAD_DIM)
    k, v = kv[:, :, 0], kv[:, :, 1]
    s = jnp.einsum('bshd,bmhd->bhsm', q, k).astype(jnp.float32) * (MEM_HEAD_DIM ** -0.5)
    p = jax.nn.softmax(s, axis=-1).astype(v.dtype)
    o = jnp.einsum('bhsm,bmhd->bshd', p, v).reshape(B, S, D)
    return o @ w_mo


def expert_choice_ffn(h, w_router, w_gate, w_up, w_down):
    B, S, D = h.shape
    n = B * S
    t = h.reshape(n, D)
    cap = EC_FACTOR * n // N_EXPERTS
    aff = jax.nn.softmax((t @ w_router).astype(jnp.float32), axis=-1)
    gates, idx = lax.top_k(aff.T, cap)

    def expert(args):
        ids, g, wg, wu, wd = args
        xe = t[ids]
        he = jax.nn.silu(xe @ wg) * (xe @ wu)
        return (he @ wd) * g[:, None].astype(t.dtype)

    ye = lax.map(expert, (idx, gates, w_gate, w_up, w_down))
    y = jnp.zeros_like(t).at[idx.reshape(-1)].add(ye.reshape(-1, D))
    return y.reshape(B, S, D)


def encoder_layer(x, mem, attn_norm, w_in, q_norm, k_norm, rpb, out_norm_a, out_norm_b, w_out,
                  xattn_norm, mem_norm, w_mq, w_mkv, w_mo, ffn_norm, w_router, w_gate, w_up, w_down):
    B, S, _ = x.shape
    h = rmsnorm(x, attn_norm)
    z = h @ w_in
    qa, ka, va, qb, kb, vb = jnp.split(z, SPLITS, axis=-1)
    ya = global_gqa(qa.reshape(B, S, N_HEADS_A, HEAD_DIM),
                    ka.reshape(B, S, N_KV_A, HEAD_DIM),
                    va.reshape(B, S, N_KV_A, HEAD_DIM), q_norm, k_norm)
    yb = neighbourhood_attn(qb.reshape(B, S, N_HEADS_B, HEAD_DIM),
                            kb.reshape(B, S, N_HEADS_B, HEAD_DIM),
                            vb.reshape(B, S, N_HEADS_B, HEAD_DIM), rpb)
    y = jnp.concatenate([rmsnorm(ya, out_norm_a), rmsnorm(yb, out_norm_b)], axis=-1)
    x = x + y @ w_out
    x = x + memory_xattn(rmsnorm(x, xattn_norm), mem, mem_norm, w_mq, w_mkv, w_mo)
    x = x + expert_choice_ffn(rmsnorm(x, ffn_norm), w_router, w_gate, w_up, w_down)
    return x


def setup_inputs(seed: int = 0) -> dict:
    key = jax.random.key(seed)
    ks = jax.random.split(key, 24)
    f32 = jnp.float32
    nrm = lambda k, shape, s: jax.random.normal(k, shape, f32) * s
    gain = lambda k, shape: 1.0 + 0.02 * jax.random.normal(k, shape, f32)
    D = D_MODEL
    return {
        'x_prompt': nrm(ks[0], (BATCH, SEQ, D), 1.0),
        'x_sample': nrm(ks[1], (DEC_BATCH, DEC_SEQ, D), 1.0),
        'mem_prompt': nrm(ks[2], (BATCH, N_MEM, D), 1.0),
        'mem_sample': nrm(ks[3], (DEC_BATCH, N_MEM, D), 1.0),
        'attn_norm': gain(ks[4], (DEPTH, D)),
        'w_in': nrm(ks[5], (DEPTH, D, D_IN), D ** -0.5),
        'q_norm': gain(ks[6], (DEPTH, HEAD_DIM)),
        'k_norm': gain(ks[7], (DEPTH, HEAD_DIM)),
        'rpb': nrm(ks[8], (DEPTH, N_HEADS_B, 2 * NB_ROWS_MAX - 1, 2 * NB_COLS - 1), 0.1),
        'out_norm_a': gain(ks[9], (DEPTH, QA_W)),
        'out_norm_b': gain(ks[10], (DEPTH, QB_W)),
        'w_out': nrm(ks[11], (DEPTH, D_MIX, D), D_MIX ** -0.5),
        'xattn_norm': gain(ks[12], (DEPTH, D)),
        'mem_norm': gain(ks[13], (DEPTH, D)),
        'w_mq': nrm(ks[14], (DEPTH, D, D), D ** -0.5),
        'w_mkv': nrm(ks[15], (DEPTH, D, 2 * D), D ** -0.5),
        'w_mo': nrm(ks[16], (DEPTH, D, D), D ** -0.5),
        'ffn_norm': gain(ks[17], (DEPTH, D)),
        'w_router': nrm(ks[18], (DEPTH, D, N_EXPERTS), D ** -0.5),
        'w_gate': nrm(ks[19], (DEPTH, N_EXPERTS, D, D_FF), D ** -0.5),
        'w_up': nrm(ks[20], (DEPTH, N_EXPERTS, D, D_FF), D ** -0.5),
        'w_down': nrm(ks[21], (DEPTH, N_EXPERTS, D_FF, D), D_FF ** -0.5),
        'final_norm': gain(ks[22], (D,)),
    }


def reference(x_prompt, x_sample, mem_prompt, mem_sample, attn_norm, w_in, q_norm, k_norm, rpb,
              out_norm_a, out_norm_b, w_out, xattn_norm, mem_norm, w_mq, w_mkv, w_mo, ffn_norm,
              w_router, w_gate, w_up, w_down, final_norm):
    xp = x_prompt
    xs = x_sample
    for l in range(DEPTH):
        xp = encoder_layer(xp, mem_prompt, attn_norm[l], w_in[l], q_norm[l], k_norm[l], rpb[l],
                           out_norm_a[l], out_norm_b[l], w_out[l], xattn_norm[l], mem_norm[l],
                           w_mq[l], w_mkv[l], w_mo[l], ffn_norm[l], w_router[l], w_gate[l],
                           w_up[l], w_down[l])
        xs = encoder_layer(xs, mem_sample, attn_norm[l], w_in[l], q_norm[l], k_norm[l], rpb[l],
                           out_norm_a[l], out_norm_b[l], w_out[l], xattn_norm[l], mem_norm[l],
                           w_mq[l], w_mkv[l], w_mo[l], ffn_norm[l], w_router[l], w_gate[l],
                           w_up[l], w_down[l])
    y_prompt = rmsnorm(xp, final_norm)
    y_sample = rmsnorm(xs, final_norm)
    return (y_prompt, y_sample)
```

```python
import functools

import jax
import jax.numpy as jnp
from jax import lax
from jax.experimental import pallas as pl
from jax.experimental.pallas import tpu as pltpu

F32 = jnp.float32
BF16 = jnp.bfloat16

D_MODEL = 1024
GRID_W = 64
HEAD_DIM = 64
N_HEADS_A = 8
N_KV_A = 2
N_HEADS_B = 8
QA_W = N_HEADS_A * HEAD_DIM
KA_W = N_KV_A * HEAD_DIM
QB_W = N_HEADS_B * HEAD_DIM
D_IN = QA_W + 2 * KA_W + 3 * QB_W
ROPE_THETA = 10000.0
NB_ROWS = 8
NB_COLS = 16
N_MEM = 256
MEM_HEADS = 4
MEM_HEAD_DIM = D_MODEL // MEM_HEADS
N_EXPERTS = 16
EC_FACTOR = 2
D_FF = 2 * D_MODEL
EPS = 1e-6
NEG = -1e30

LANES = 128
VMEM_LIMIT = 56 * 1024 * 1024

NT_DIMS = (((1,), (1,)), ((), ()))
TN_DIMS = (((0,), (0,)), ((), ()))


def _cparams(*sem):
    return pltpu.CompilerParams(dimension_semantics=sem, vmem_limit_bytes=VMEM_LIMIT)


def _rms(x, g):
    return x * lax.rsqrt(jnp.mean(x * x, axis=-1, keepdims=True) + EPS) * g


def _split_bf16(x):
    hi = x.astype(BF16)
    lo = (x - hi.astype(F32)).astype(BF16)
    return hi, lo


def _head_norm_rope(z, gain, cos, sin_signed, blockdiag):
    w = z.shape[-1]
    hi, lo = _split_bf16(z * z)
    ss = (jnp.dot(hi, blockdiag, preferred_element_type=F32)
          + jnp.dot(lo, blockdiag, preferred_element_type=F32))
    y = z * lax.rsqrt(ss * (1.0 / HEAD_DIM) + EPS) * gain
    lane = lax.broadcasted_iota(jnp.int32, y.shape, 1)
    first = (lane % 32) < 16
    rot = jnp.where(first, pltpu.roll(y, w - 16, axis=1), pltpu.roll(y, 16, axis=1))
    return y * cos + rot * sin_signed


def _inproj_kernel(x_ref, g_ref, w_ref, wvt_ref, cos_ref, sin_ref, qg_ref, kg_ref, bdq_ref, bdk_ref,
                   qa_ref, ka_ref, vt_ref, qb_ref, kb_ref, vb_ref):
    h = _rms(x_ref[...], g_ref[...]).astype(BF16)

    def proj(lo, hi):
        return jnp.dot(h, w_ref[:, lo:hi], preferred_element_type=F32)

    cos = cos_ref[...]
    sin = sin_ref[...]
    q = _head_norm_rope(proj(0, QA_W), qg_ref[...], cos, sin, bdq_ref[...])
    k = _head_norm_rope(proj(QA_W, QA_W + KA_W), kg_ref[...], cos[:, :KA_W], sin[:, :KA_W],
                        bdk_ref[...])
    vt = lax.dot_general(wvt_ref[...], h, NT_DIMS, preferred_element_type=F32)

    lane = lax.broadcasted_iota(jnp.int32, (q.shape[0], LANES), 1)
    group = N_HEADS_A // N_KV_A
    for i in range(N_HEADS_A):
        src = q[:, LANES * (i // 2):LANES * (i // 2 + 1)]
        kv = i // group
        if (i % 2) != kv:
            src = pltpu.roll(src, HEAD_DIM, axis=1)
        keep = (lane >= HEAD_DIM * kv) & (lane < HEAD_DIM * (kv + 1))
        qa_ref[:, LANES * i:LANES * (i + 1)] = jnp.where(keep, src, 0.0).astype(BF16)
    ka_ref[...] = k.astype(BF16)
    vt_ref[...] = vt.astype(BF16)

    o = QA_W + 2 * KA_W
    qb_ref[...] = (proj(o, o + QB_W) * (HEAD_DIM ** -0.5)).astype(BF16)
    kb_ref[...] = proj(o + QB_W, o + 2 * QB_W).astype(BF16)
    vb_ref[...] = proj(o + 2 * QB_W, o + 3 * QB_W).astype(BF16)


def _rope_tables(seq_len):
    half = HEAD_DIM // 2
    inv = ROPE_THETA ** (-jnp.arange(0, half, 2, dtype=F32) / half)
    t = jnp.arange(seq_len)
    row = (t // GRID_W).astype(F32)
    col = (t % GRID_W).astype(F32)
    ar = row[:, None] * inv[None]
    ac = col[:, None] * inv[None]
    ang = jnp.concatenate([ar, ar, ac, ac], axis=-1)
    sign = jnp.where((jnp.arange(HEAD_DIM) % 32) < 16, -1.0, 1.0).astype(F32)
    cos = jnp.tile(jnp.cos(ang), (1, N_HEADS_A))
    sin = jnp.tile(jnp.sin(ang) * sign[None], (1, N_HEADS_A))
    return cos, sin


def _blockdiag(width):
    i = jnp.arange(width) // HEAD_DIM
    return (i[:, None] == i[None, :]).astype(BF16)


def _inproj(x, attn_norm, w_in_bf, q_norm, k_norm, tm):
    b, s, d = x.shape
    cos, sin = _rope_tables(s)
    qg = (jnp.tile(q_norm, N_HEADS_A) * (HEAD_DIM ** -0.5))[None]
    kg = jnp.tile(k_norm, N_KV_A)[None]
    const = lambda shape: pl.BlockSpec(shape, lambda i, j: (0,) * len(shape))
    tok = lambda w: pl.BlockSpec((None, tm, w), lambda i, j: (j, i, 0))
    out_shapes = (
        jax.ShapeDtypeStruct((b, s, N_HEADS_A * LANES), BF16),
        jax.ShapeDtypeStruct((b, s, KA_W), BF16),
        jax.ShapeDtypeStruct((b, KA_W, s), BF16),
        jax.ShapeDtypeStruct((b, s, QB_W), BF16),
        jax.ShapeDtypeStruct((b, s, QB_W), BF16),
        jax.ShapeDtypeStruct((b, s, QB_W), BF16),
    )
    return pl.pallas_call(
        _inproj_kernel,
        grid=(s // tm, b),
        in_specs=[
            tok(d),
            const((1, d)),
            const((d, D_IN)),
            const((KA_W, d)),
            pl.BlockSpec((tm, QA_W), lambda i, j: (i, 0)),
            pl.BlockSpec((tm, QA_W), lambda i, j: (i, 0)),
            const((1, QA_W)),
            const((1, KA_W)),
            const((QA_W, QA_W)),
            const((KA_W, KA_W)),
        ],
        out_specs=(
            tok(N_HEADS_A * LANES),
            tok(KA_W),
            pl.BlockSpec((None, KA_W, tm), lambda i, j: (j, 0, i)),
            tok(QB_W), tok(QB_W), tok(QB_W),
        ),
        out_shape=out_shapes,
        compiler_params=_cparams("arbitrary", "arbitrary"),
        name="inproj",
    )(x, attn_norm[None], w_in_bf, w_in_bf[:, QA_W + KA_W:QA_W + 2 * KA_W].T, cos, sin, qg, kg, _blockdiag(QA_W), _blockdiag(KA_W))


def _gattn_kernel(q_ref, k_ref, vt_ref, o_ref, *, kv_tile):
    tq = q_ref.shape[0]
    group = N_HEADS_A // N_KV_A
    qb = q_ref[...]
    q4 = jnp.concatenate([qb[:, LANES * i:LANES * (i + 1)] for i in range(group)], axis=0)
    n_q = group * tq
    n_kv = k_ref.shape[0] // kv_tile

    def body(j, carry):
        m, l, acc = carry
        off = pl.multiple_of(j * kv_tile, kv_tile)
        s = lax.dot_general(k_ref[pl.ds(off, kv_tile), :], q4, NT_DIMS,
                            preferred_element_type=F32)
        m_new = jnp.maximum(m, jnp.max(s, axis=0, keepdims=True))
        alpha = jnp.exp(m - m_new)
        p = jnp.exp(s - m_new)
        l = l * alpha + jnp.sum(p, axis=0, keepdims=True)
        pv = jnp.dot(vt_ref[:, pl.ds(off, kv_tile)], p.astype(BF16), preferred_element_type=F32)
        return m_new, l, acc * alpha + pv

    init = (jnp.full((1, n_q), NEG, F32), jnp.zeros((1, n_q), F32), jnp.zeros((HEAD_DIM, n_q), F32))
    _, l, acc = lax.fori_loop(0, n_kv, body, init)
    ot = acc / l
    stacked = jnp.concatenate([ot[:, tq * i:tq * (i + 1)] for i in range(group)], axis=0)
    o_ref[...] = stacked.T.astype(o_ref.dtype)


def _gattn(qa, ka, vt, tq, kv_tile):
    b, s, _ = ka.shape
    group = N_HEADS_A // N_KV_A
    return pl.pallas_call(
        functools.partial(_gattn_kernel, kv_tile=kv_tile),
        grid=(b, N_KV_A, s // tq),
        in_specs=[
            pl.BlockSpec((None, tq, group * LANES), lambda bi, g, i: (bi, i, g)),
            pl.BlockSpec((None, s, KA_W), lambda bi, g, i: (bi, 0, 0)),
            pl.BlockSpec((None, HEAD_DIM, s), lambda bi, g, i: (bi, g, 0)),
        ],
        out_specs=pl.BlockSpec((None, tq, group * HEAD_DIM), lambda bi, g, i: (bi, i, g)),
        out_shape=jax.ShapeDtypeStruct((b, s, QA_W), BF16),
        compiler_params=_cparams("arbitrary", "arbitrary", "arbitrary"),
        name="gattn",
    )(qa, ka, vt)


def _nattn_kernel(q_ref, k_ref, v_ref, tbl_ref, o_ref, *, rows):
    r = pl.program_id(1)
    rs = jnp.clip(r - NB_ROWS // 2, 0, rows - NB_ROWS)
    off = pl.multiple_of(rs * GRID_W, GRID_W)
    win = NB_ROWS * GRID_W
    lane = lax.broadcasted_iota(jnp.int32, (GRID_W, LANES), 1)
    low = lane < HEAD_DIM
    for pp in range(N_HEADS_B // 2):
        cols = slice(LANES * pp, LANES * (pp + 1))
        qp = q_ref[:, cols]
        kp = k_ref[pl.ds(off, win), cols]
        vp = v_ref[pl.ds(off, win), cols]
        outs = []
        for hh in range(2):
            qh = jnp.where(low if hh == 0 else jnp.logical_not(low), qp, jnp.zeros_like(qp))
            s = lax.dot_general(qh, kp, NT_DIMS, preferred_element_type=F32) + tbl_ref[2 * pp + hh]
            m = jnp.max(s, axis=-1, keepdims=True)
            p = jnp.exp(s - m)
            l = jnp.sum(p, axis=-1, keepdims=True)
            outs.append(jnp.dot(p.astype(BF16), vp, preferred_element_type=F32) / l)
        o_ref[:, cols] = jnp.where(low, outs[0], outs[1]).astype(o_ref.dtype)


def _na_bias_table(rpb):
    qc = jnp.arange(GRID_W)
    kc = jnp.arange(GRID_W)
    cstart = jnp.clip(qc - NB_COLS // 2, 0, GRID_W - NB_COLS)
    valid = (kc[None, :] >= cstart[:, None]) & (kc[None, :] < cstart[:, None] + NB_COLS)
    dc = jnp.clip(kc[None, :] - qc[:, None] + (NB_COLS - 1), 0, 2 * NB_COLS - 2)
    cls = jnp.arange(NB_ROWS)
    j = jnp.arange(NB_ROWS)
    dr = j[None, :] - cls[:, None] + (NB_ROWS - 1)
    bias = rpb[:, dr[:, :, None, None], dc[None, None]]
    bias = jnp.where(valid[None, None, None], bias, NEG)
    bias = bias.transpose(1, 0, 3, 2, 4)
    return bias.reshape(NB_ROWS, N_HEADS_B, GRID_W, NB_ROWS * GRID_W).astype(F32)


def _nattn(qb, kb, vb, rpb):
    b, s, w = qb.shape
    rows = s // GRID_W
    assert rows >= NB_ROWS
    tbl = _na_bias_table(rpb)

    def cls_map(bi, r):
        return (r - jnp.clip(r - NB_ROWS // 2, 0, rows - NB_ROWS), 0, 0, 0)

    return pl.pallas_call(
        functools.partial(_nattn_kernel, rows=rows),
        grid=(b, rows),
        in_specs=[
            pl.BlockSpec((None, GRID_W, w), lambda bi, r: (bi, r, 0)),
            pl.BlockSpec((None, s, w), lambda bi, r: (bi, 0, 0)),
            pl.BlockSpec((None, s, w), lambda bi, r: (bi, 0, 0)),
            pl.BlockSpec((None, N_HEADS_B, GRID_W, NB_ROWS * GRID_W), cls_map),
        ],
        out_specs=pl.BlockSpec((None, GRID_W, w), lambda bi, r: (bi, r, 0)),
        out_shape=jax.ShapeDtypeStruct((b, s, w), BF16),
        compiler_params=_cparams("arbitrary", "arbitrary"),
        name="nattn",
    )(qb, kb, vb, tbl)


def _memkv_kernel(mem_ref, g_ref, w_ref, k_ref, v_ref):
    m = _rms(mem_ref[...], g_ref[...]).astype(BF16)
    k_ref[...] = jnp.dot(m, w_ref[:, :D_MODEL], preferred_element_type=F32).astype(BF16)
    v_ref[...] = jnp.dot(m, w_ref[:, D_MODEL:], preferred_element_type=F32).astype(BF16)


def _memkv(mem, mem_norm, w_mkv_bf):
    b, m, d = mem.shape
    blk = pl.BlockSpec((None, m, d), lambda i: (i, 0, 0))
    return pl.pallas_call(
        _memkv_kernel,
        grid=(b,),
        in_specs=[blk, pl.BlockSpec((1, d), lambda i: (0, 0)), pl.BlockSpec((d, 2 * d), lambda i: (0, 0))],
        out_specs=(blk, blk),
        out_shape=(jax.ShapeDtypeStruct((b, m, d), BF16),) * 2,
        compiler_params=_cparams("arbitrary"),
        name="memkv",
    )(mem, mem_norm[None], w_mkv_bf)


def _mix_kernel(x_ref, ya_ref, yb_ref, mk_ref, mv_ref, ga_ref, gb_ref, wo_ref, gx_ref, wq_ref, wmo_ref,
                gf_ref, wr_hl_ref, wr_h_ref, x2_ref, h3_ref, aff_ref):
    na = _rms(ya_ref[...].astype(F32), ga_ref[...]).astype(BF16)
    nb = _rms(yb_ref[...].astype(F32), gb_ref[...]).astype(BF16)
    x1 = (x_ref[...]
          + jnp.dot(na, wo_ref[:QA_W, :], preferred_element_type=F32)
          + jnp.dot(nb, wo_ref[QA_W:, :], preferred_element_type=F32))

    h2 = _rms(x1, gx_ref[...]).astype(BF16)
    q = (jnp.dot(h2, wq_ref[...], preferred_element_type=F32) * (MEM_HEAD_DIM ** -0.5)).astype(BF16)
    heads = []
    for hh in range(MEM_HEADS):
        cols = slice(MEM_HEAD_DIM * hh, MEM_HEAD_DIM * (hh + 1))
        s = lax.dot_general(q[:, cols], mk_ref[:, cols], NT_DIMS, preferred_element_type=F32)
        m = jnp.max(s, axis=-1, keepdims=True)
        p = jnp.exp(s - m)
        l = jnp.sum(p, axis=-1, keepdims=True)
        heads.append((jnp.dot(p.astype(BF16), mv_ref[:, cols], preferred_element_type=F32) / l).astype(BF16))
    o = jnp.concatenate(heads, axis=1)
    x2 = x1 + jnp.dot(o, wmo_ref[...], preferred_element_type=F32)
    x2_ref[...] = x2

    h3 = _rms(x2, gf_ref[...])
    h3_ref[...] = h3
    hi, lo = _split_bf16(h3)
    r1 = lax.dot_general(wr_hl_ref[...], hi, NT_DIMS, preferred_element_type=F32)
    r2 = lax.dot_general(wr_h_ref[...], lo, NT_DIMS, preferred_element_type=F32)
    logits = r1[:N_EXPERTS] + r1[N_EXPERTS:] + r2
    e = jnp.exp(logits - jnp.max(logits, axis=0, keepdims=True))
    aff_ref[...] = e / jnp.sum(e, axis=0, keepdims=True)


def _mix(x, ya, yb, mk, mv, out_norm_a, out_norm_b, w_out_bf, xattn_norm, w_mq_bf, w_mo_bf, ffn_norm,
         w_router, tm):
    b, s, d = x.shape
    n = b * s
    tiles_per_seq = s // tm
    wr_t = w_router.T
    wr_hi, wr_lo = _split_bf16(wr_t)
    wr_hl = jnp.concatenate([wr_hi, wr_lo], axis=0)
    const = lambda shape: pl.BlockSpec(shape, lambda i: (0,) * len(shape))
    tok = lambda w: pl.BlockSpec((tm, w), lambda i: (i, 0))
    memb = pl.BlockSpec((None, N_MEM, d), lambda i: (i // tiles_per_seq, 0, 0))
    return pl.pallas_call(
        _mix_kernel,
        grid=(n // tm,),
        in_specs=[
            tok(d), tok(QA_W), tok(QB_W), memb, memb,
            const((1, QA_W)), const((1, QB_W)), const((d, d)),
            const((1, d)), const((d, d)), const((d, d)),
            const((1, d)), const((2 * N_EXPERTS, d)), const((N_EXPERTS, d)),
        ],
        out_specs=(tok(d), tok(d), pl.BlockSpec((N_EXPERTS, tm), lambda i: (0, i))),
        out_shape=(
            jax.ShapeDtypeStruct((n, d), F32),
            jax.ShapeDtypeStruct((n, d), F32),
            jax.ShapeDtypeStruct((N_EXPERTS, n), F32),
        ),
        compiler_params=_cparams("arbitrary"),
        name="mix",
    )(x.reshape(n, d), ya.reshape(n, QA_W), yb.reshape(n, QB_W), mk, mv, out_norm_a[None], out_norm_b[None],
      w_out_bf, xattn_norm[None], w_mq_bf, w_mo_bf, ffn_norm[None], wr_hl, wr_hi)


def _front(x, mem, p, *, tm_in=512, tq=128, kv_tile=512, tm_mix=256):
    qa, ka, vt, qb, kb, vb = _inproj(x, p["attn_norm"], p["w_in"], p["q_norm"], p["k_norm"], tm_in)
    ya = _gattn(qa, ka, vt, tq, kv_tile)
    yb = _nattn(qb, kb, vb, p["rpb"])
    mk, mv = _memkv(mem, p["mem_norm"], p["w_mkv"])
    return _mix(x, ya, yb, mk, mv, p["out_norm_a"], p["out_norm_b"], p["w_out"], p["xattn_norm"],
                p["w_mq"], p["w_mo"], p["ffn_norm"], p["w_router"], tm_mix)


def _tri(n, strict_lower=False):
    i = lax.broadcasted_iota(jnp.int32, (n, n), 0)
    j = lax.broadcasted_iota(jnp.int32, (n, n), 1)
    return ((j < i) if strict_lower else (i <= j)).astype(BF16)


def _select_kernel(aff_ref, slot_ref, sel_ref, off_ref, *, cap):
    n_e, rows, _ = aff_ref.shape

    def bits_of(e):
        return lax.bitcast_convert_type(aff_ref[e], jnp.int32)

    def step(i, prefixes):
        bit = jnp.left_shift(jnp.int32(1), 30 - i)
        out = []
        for e in range(n_e):
            cand = prefixes[e] | bit
            cnt = jnp.sum(jnp.where(bits_of(e) >= cand, 1.0, 0.0), axis=(0, 1), keepdims=True)
            out.append(jnp.where(cnt >= float(cap), cand, prefixes[e]))
        return tuple(out)

    thr = lax.fori_loop(0, 31, step, tuple(jnp.zeros((1, 1), jnp.int32) for _ in range(n_e)))

    u_lane = _tri(LANES)
    u_rows = _tri(rows)
    l_rows = _tri(rows, strict_lower=True)
    ones_l = jnp.ones((LANES, LANES), BF16)
    ones_8 = jnp.ones((8, LANES), BF16)

    def prefix(mask_bf):
        within = jnp.dot(mask_bf, u_lane, preferred_element_type=F32)
        row_tot = jnp.dot(mask_bf, ones_l, preferred_element_type=F32)
        before = jnp.dot(l_rows, row_tot.astype(BF16), preferred_element_type=F32)
        return within, before

    for e in range(n_e):
        b = bits_of(e)
        gt = b > thr[e]
        eq = b == thr[e]
        need = float(cap) - jnp.sum(jnp.where(gt, 1.0, 0.0), axis=(0, 1), keepdims=True)
        eq_f = jnp.where(eq, 1.0, 0.0)
        within, before = prefix(eq_f.astype(BF16))
        sel = gt | (eq & ((before + within - eq_f) < need))
        sel_f = jnp.where(sel, 1.0, 0.0)
        sel_bf = sel_f.astype(BF16)
        within, before = prefix(sel_bf)
        rank = (before + within - 1.0).astype(jnp.int32)
        slot_ref[e] = jnp.where(sel, rank + e * cap, -1)
        sel_ref[e] = sel_bf
        cnt_row = lax.dot_general(ones_8, sel_bf, NT_DIMS, preferred_element_type=F32)
        end_row = jnp.dot(cnt_row.astype(BF16), u_rows, preferred_element_type=F32)
        off_ref[e] = end_row - cnt_row


def _select(aff3, cap):
    n_e, rows, _ = aff3.shape
    return pl.pallas_call(
        functools.partial(_select_kernel, cap=cap),
        out_shape=(
            jax.ShapeDtypeStruct((n_e, rows, LANES), jnp.int32),
            jax.ShapeDtypeStruct((n_e, rows, LANES), BF16),
            jax.ShapeDtypeStruct((n_e, 8, rows), F32),
        ),
        compiler_params=pltpu.CompilerParams(vmem_limit_bytes=VMEM_LIMIT),
        name="select",
    )(aff3)


def _compact_kernel(sel_ref, aff_ref, idx_ref, gate_ref, *, jt):
    rows = sel_ref.shape[0]
    sel = sel_ref[...]
    ones_8 = jnp.ones((8, LANES), BF16)
    cnt_row = lax.dot_general(ones_8, sel, NT_DIMS, preferred_element_type=F32)
    end_row = jnp.dot(cnt_row.astype(BF16), _tri(rows), preferred_element_type=F32)
    row_end = end_row[0:1, :]
    row_off = row_end - cnt_row[0:1, :]

    j0 = pl.program_id(1) * jt
    j_r = (j0 + lax.broadcasted_iota(jnp.int32, (jt, rows), 0)).astype(F32)
    below = jnp.where(row_end <= j_r, 1.0, 0.0).astype(BF16)
    hit = jnp.where((row_end > j_r) & (row_off <= j_r), 1.0, 0.0).astype(BF16)

    cnt_b = jnp.dot(sel, jnp.ones((LANES, LANES), BF16), preferred_element_type=F32)
    off_b = jnp.dot(below, cnt_b.astype(BF16), preferred_element_type=F32)
    sel_row = jnp.dot(hit, sel, preferred_element_type=F32)
    within = jnp.dot(sel_row.astype(BF16), _tri(LANES), preferred_element_type=F32)
    j_l = (j0 + lax.broadcasted_iota(jnp.int32, (jt, LANES), 0)).astype(F32)
    pick = (sel_row > 0.5) & (within == (j_l - off_b + 1.0))
    pick_bf = jnp.where(pick, 1.0, 0.0).astype(BF16)

    sub = lax.broadcasted_iota(jnp.int32, (8, LANES), 0)
    lane_id = jnp.where(sub == 0, lax.broadcasted_iota(jnp.int32, (8, LANES), 1), 0).astype(BF16)
    lane_pos = lax.dot_general(lane_id, pick_bf, NT_DIMS, preferred_element_type=F32)
    sub_r = lax.broadcasted_iota(jnp.int32, (8, rows), 0)
    r_id = lax.broadcasted_iota(jnp.int32, (8, rows), 1)
    r_split = jnp.where(sub_r == 0, r_id // 16, jnp.where(sub_r == 1, r_id % 16, 0)).astype(BF16)
    row_pos = lax.dot_general(r_split, hit, NT_DIMS, preferred_element_type=F32)
    tok = (row_pos[0:1, :] * 16.0 + row_pos[1:2, :]) * float(LANES) + lane_pos[0:1, :]
    idx_ref[...] = tok.astype(jnp.int32)

    a = aff_ref[...]
    a0 = a.astype(BF16)
    a1 = (a - a0.astype(F32)).astype(BF16)
    a2 = (a - a0.astype(F32) - a1.astype(F32)).astype(BF16)
    gate = jnp.zeros((1, jt), F32)
    for piece in (a0, a1, a2):
        g = jnp.dot(hit, piece, preferred_element_type=F32)
        g = jnp.where(pick, g, 0.0).astype(BF16)
        gate = gate + lax.dot_general(ones_8, g, NT_DIMS, preferred_element_type=F32)[0:1, :]
    gate_ref[...] = gate


def _compact(sel, aff3, cap, jt):
    n_e, rows, _ = sel.shape
    blk = pl.BlockSpec((None, rows, LANES), lambda e, j: (e, 0, 0))
    out_blk = pl.BlockSpec((None, 1, jt), lambda e, j: (e, 0, j))
    return pl.pallas_call(
        functools.partial(_compact_kernel, jt=jt),
        grid=(n_e, cap // jt),
        in_specs=[blk, blk],
        out_specs=(out_blk, out_blk),
        out_shape=(jax.ShapeDtypeStruct((n_e, 1, cap), jnp.int32), jax.ShapeDtypeStruct((n_e, 1, cap), F32)),
        compiler_params=_cparams("arbitrary", "arbitrary"),
        name="compact",
    )(sel, aff3)


def _ffn_kernel(idx_ref, gate_ref, h_hbm, wg_ref, wu_ref, wd_ref, o_ref, xbuf, sem, *, fc):
    ts = xbuf.shape[0]

    def row_copy(j, src_row):
        return pltpu.make_async_copy(h_hbm.at[pl.ds(src_row, 1), :], xbuf.at[pl.ds(j, 1), :], sem)

    def issue(j, c):
        row_copy(j, idx_ref[0, j]).start()
        return c

    def drain(j, c):
        row_copy(j, 0).wait()
        return c

    lax.fori_loop(0, ts, issue, 0)
    lax.fori_loop(0, ts, drain, 0)

    xe = xbuf[...].astype(BF16)
    acc = jnp.zeros((ts, D_MODEL), F32)
    for c in range(D_FF // fc):
        cols = slice(fc * c, fc * (c + 1))
        g = jnp.dot(xe, wg_ref[:, cols], preferred_element_type=F32)
        u = jnp.dot(xe, wu_ref[:, cols], preferred_element_type=F32)
        he = (g / (1.0 + jnp.exp(-g)) * u).astype(BF16)
        acc = acc + jnp.dot(he, wd_ref[cols, :], preferred_element_type=F32)
    gate_col = jnp.broadcast_to(gate_ref[...], (LANES, ts)).T[:, 0:1]
    o_ref[...] = (acc * gate_col).astype(o_ref.dtype)


def _ffn(idx, gate, h3, wg, wu, wd, ts, fc):
    n_e, _, cap = idx.shape
    d = h3.shape[-1]
    idx4 = idx.reshape(n_e, cap // ts, 1, ts)
    return pl.pallas_call(
        functools.partial(_ffn_kernel, fc=fc),
        grid=(n_e, cap // ts),
        in_specs=[
            pl.BlockSpec((None, None, 1, ts), lambda e, s: (e, s, 0, 0), memory_space=pltpu.SMEM),
            pl.BlockSpec((None, 1, ts), lambda e, s: (e, 0, s)),
            pl.BlockSpec(memory_space=pl.ANY),
            pl.BlockSpec((None, d, D_FF), lambda e, s: (e, 0, 0)),
            pl.BlockSpec((None, d, D_FF), lambda e, s: (e, 0, 0)),
            pl.BlockSpec((None, D_FF, d), lambda e, s: (e, 0, 0)),
        ],
        out_specs=pl.BlockSpec((ts, d), lambda e, s: (e * (cap // ts) + s, 0)),
        out_shape=jax.ShapeDtypeStruct((n_e * cap, d), BF16),
        scratch_shapes=[pltpu.VMEM((ts, d), F32), pltpu.SemaphoreType.DMA(())],
        compiler_params=_cparams("arbitrary", "arbitrary"),
        name="ffn",
    )(idx4, gate, h3, wg, wu, wd)


COMBINE_WIN = 64


def _combine_kernel(base_ref, rounds_ref, x_ref, slot_ref, ye_hbm, g_ref, o_ref, buf, sem):
    n_e, t = slot_ref.shape
    total = ye_hbm.shape[0]
    tt = pl.program_id(0)
    w_iota = lax.broadcasted_iota(jnp.int32, (COMBINE_WIN, t), 0)

    def one_round(r, acc):
        lows = [base_ref[e, tt] + r * COMBINE_WIN for e in range(n_e)]
        starts = [pl.multiple_of(jnp.minimum(lo, total - COMBINE_WIN), 16) for lo in lows]

        def win_copy(e):
            return pltpu.make_async_copy(ye_hbm.at[pl.ds(starts[e], COMBINE_WIN), :],
                                         buf.at[pl.ds(e * COMBINE_WIN, COMBINE_WIN), :], sem)

        for e in range(n_e):
            win_copy(e).start()
        for e in range(n_e):
            win_copy(e).wait()
        onehot_t = jnp.concatenate(
            [jnp.where((slot_ref[e:e + 1, :] == starts[e] + w_iota) & (slot_ref[e:e + 1, :] >= lows[e]),
                       1.0, 0.0).astype(BF16) for e in range(n_e)],
            axis=0)
        return acc + lax.dot_general(onehot_t, buf[...], TN_DIMS, preferred_element_type=F32)

    y = lax.fori_loop(0, rounds_ref[tt], one_round, jnp.zeros((t, D_MODEL), F32))
    o_ref[...] = _rms(x_ref[...] + y, g_ref[...])


def _combine(x2, slot, ye, base, rounds, final_norm, t):
    n, d = x2.shape
    n_e = slot.shape[0]
    return pl.pallas_call(
        _combine_kernel,
        grid_spec=pltpu.PrefetchScalarGridSpec(
            num_scalar_prefetch=2,
            grid=(n // t,),
            in_specs=[
                pl.BlockSpec((t, d), lambda i, *_: (i, 0)),
                pl.BlockSpec((n_e, t), lambda i, *_: (0, i)),
                pl.BlockSpec(memory_space=pl.ANY),
                pl.BlockSpec((1, d), lambda i, *_: (0, 0)),
            ],
            out_specs=pl.BlockSpec((t, d), lambda i, *_: (i, 0)),
            scratch_shapes=[pltpu.VMEM((n_e * COMBINE_WIN, d), BF16), pltpu.SemaphoreType.DMA(())],
        ),
        out_shape=jax.ShapeDtypeStruct((n, d), F32),
        compiler_params=_cparams("arbitrary"),
        name="combine",
    )(base, rounds, x2, slot, ye, final_norm[None])


def _moe(x2, h3, aff, p, *, jt=256, ts=512, fc=512, t=256):
    n, d = x2.shape
    n_e = aff.shape[0]
    cap = EC_FACTOR * n // n_e
    rows = n // LANES
    aff3 = aff.reshape(n_e, rows, LANES)
    slot, sel, off = _select(aff3, cap)
    idx, gate = _compact(sel, aff3, cap, min(jt, cap))
    ye = _ffn(idx, gate, h3, p["w_gate"], p["w_up"], p["w_down"], min(ts, cap), fc)

    rows_per_tile = t // LANES
    first = off[:, 0, ::rows_per_tile].astype(jnp.int32) + cap * jnp.arange(n_e, dtype=jnp.int32)[:, None]
    nxt = jnp.concatenate([first[:, 1:], (cap * (jnp.arange(n_e, dtype=jnp.int32) + 1))[:, None]], axis=1)
    base = (first // 16) * 16
    rounds = jnp.max((nxt - base + COMBINE_WIN - 1) // COMBINE_WIN, axis=0).astype(jnp.int32)
    rounds = jnp.where(jnp.max(nxt - first, axis=0) > 0, rounds, 0)
    return _combine(x2, slot.reshape(n_e, n), ye, base, rounds, p["final_norm"], t)


def kernel(x_prompt, x_sample, mem_prompt, mem_sample, attn_norm, w_in, q_norm, k_norm, rpb, out_norm_a,
           out_norm_b, w_out, xattn_norm, mem_norm, w_mq, w_mkv, w_mo, ffn_norm, w_router, w_gate, w_up,
           w_down, final_norm):
    assert attn_norm.shape[0] == 1, "single-layer encoder"
    p = dict(attn_norm=attn_norm[0], w_in=w_in[0].astype(BF16), q_norm=q_norm[0], k_norm=k_norm[0],
             rpb=rpb[0], out_norm_a=out_norm_a[0], out_norm_b=out_norm_b[0], w_out=w_out[0].astype(BF16),
             xattn_norm=xattn_norm[0], mem_norm=mem_norm[0], w_mq=w_mq[0].astype(BF16),
             w_mkv=w_mkv[0].astype(BF16), w_mo=w_mo[0].astype(BF16), ffn_norm=ffn_norm[0],
             w_router=w_router[0], w_gate=w_gate[0].astype(BF16), w_up=w_up[0].astype(BF16),
             w_down=w_down[0].astype(BF16), final_norm=final_norm)
    outs = []
    for x, mem in ((x_prompt, mem_prompt), (x_sample, mem_sample)):
        x2, h3, aff = _front(x, mem, p)
        outs.append(_moe(x2, h3, aff, p).reshape(x.shape))
    return tuple(outs)
```

```python
import functools

import jax
import jax.numpy as jnp
from jax import lax
from jax.experimental import pallas as pl
from jax.experimental.pallas import tpu as pltpu

F32 = jnp.float32
BF16 = jnp.bfloat16

D_MODEL = 1024
GRID_W = 64
HEAD_DIM = 64
N_HEADS_A = 8
N_KV_A = 2
N_HEADS_B = 8
QA_W = N_HEADS_A * HEAD_DIM
KA_W = N_KV_A * HEAD_DIM
QB_W = N_HEADS_B * HEAD_DIM
D_IN = QA_W + 2 * KA_W + 3 * QB_W
ROPE_THETA = 10000.0
NB_ROWS = 8
NB_COLS = 16
N_MEM = 256
MEM_HEADS = 4
MEM_HEAD_DIM = D_MODEL // MEM_HEADS
N_EXPERTS = 16
EC_FACTOR = 2
D_FF = 2 * D_MODEL
EPS = 1e-6
NEG = -1e30
LOG2E = 1.4426950408889634

LANES = 128
VMEM_LIMIT = 56 * 1024 * 1024

NT_DIMS = (((1,), (1,)), ((), ()))
TN_DIMS = (((0,), (0,)), ((), ()))


def _cparams(*sem):
    return pltpu.CompilerParams(dimension_semantics=sem, vmem_limit_bytes=VMEM_LIMIT)


def _rms(x, g):
    return x * lax.rsqrt(jnp.mean(x * x, axis=-1, keepdims=True) + EPS) * g


def _split_bf16(x):
    hi = x.astype(BF16)
    lo = (x - hi.astype(F32)).astype(BF16)
    return hi, lo


def _head_norm_rope(z, gain, cos, sin_signed, blockdiag):
    w = z.shape[-1]
    hi, lo = _split_bf16(z * z)
    ss = (jnp.dot(hi, blockdiag, preferred_element_type=F32)
          + jnp.dot(lo, blockdiag, preferred_element_type=F32))
    y = z * lax.rsqrt(ss * (1.0 / HEAD_DIM) + EPS) * gain
    lane = lax.broadcasted_iota(jnp.int32, y.shape, 1)
    first = (lane % 32) < 16
    rot = jnp.where(first, pltpu.roll(y, w - 16, axis=1), pltpu.roll(y, 16, axis=1))
    return y * cos + rot * sin_signed


def _inproj_kernel(x_ref, g_ref, w_ref, wvt_ref, cos_ref, sin_ref, qg_ref, kg_ref, bdq_ref, bdk_ref,
                   qa_ref, ka_ref, vt_ref, qb_ref, kb_ref, vb_ref):
    h = _rms(x_ref[...], g_ref[...]).astype(BF16)

    def proj(lo, hi):
        return jnp.dot(h, w_ref[:, lo:hi], preferred_element_type=F32)

    cos = cos_ref[...]
    sin = sin_ref[...]
    q = _head_norm_rope(proj(0, QA_W), qg_ref[...], cos, sin, bdq_ref[...])
    k = _head_norm_rope(proj(QA_W, QA_W + KA_W), kg_ref[...], cos[:, :KA_W], sin[:, :KA_W],
                        bdk_ref[...])
    vt = lax.dot_general(wvt_ref[...], h, NT_DIMS, preferred_element_type=F32)

    lane = lax.broadcasted_iota(jnp.int32, (q.shape[0], LANES), 1)
    group = N_HEADS_A // N_KV_A
    for i in range(N_HEADS_A):
        src = q[:, LANES * (i // 2):LANES * (i // 2 + 1)]
        kv = i // group
        if (i % 2) != kv:
            src = pltpu.roll(src, HEAD_DIM, axis=1)
        keep = (lane >= HEAD_DIM * kv) & (lane < HEAD_DIM * (kv + 1))
        qa_ref[:, LANES * i:LANES * (i + 1)] = jnp.where(keep, src, 0.0).astype(BF16)
    ka_ref[...] = k.astype(BF16)
    vt_ref[...] = vt.astype(BF16)

    o = QA_W + 2 * KA_W
    qb_ref[...] = (proj(o, o + QB_W) * (HEAD_DIM ** -0.5)).astype(BF16)
    kb_ref[...] = proj(o + QB_W, o + 2 * QB_W).astype(BF16)
    vb_ref[...] = proj(o + 2 * QB_W, o + 3 * QB_W).astype(BF16)


def _rope_tables(seq_len):
    half = HEAD_DIM // 2
    inv = ROPE_THETA ** (-jnp.arange(0, half, 2, dtype=F32) / half)
    t = jnp.arange(seq_len)
    row = (t // GRID_W).astype(F32)
    col = (t % GRID_W).astype(F32)
    ar = row[:, None] * inv[None]
    ac = col[:, None] * inv[None]
    ang = jnp.concatenate([ar, ar, ac, ac], axis=-1)
    sign = jnp.where((jnp.arange(HEAD_DIM) % 32) < 16, -1.0, 1.0).astype(F32)
    cos = jnp.tile(jnp.cos(ang), (1, N_HEADS_A))
    sin = jnp.tile(jnp.sin(ang) * sign[None], (1, N_HEADS_A))
    return cos, sin


def _blockdiag(width):
    i = jnp.arange(width) // HEAD_DIM
    return (i[:, None] == i[None, :]).astype(BF16)


def _inproj(x, attn_norm, w_in_bf, q_norm, k_norm, tm):
    b, s, d = x.shape
    cos, sin = _rope_tables(s)
    qg = (jnp.tile(q_norm, N_HEADS_A) * (HEAD_DIM ** -0.5 * LOG2E))[None]
    kg = jnp.tile(k_norm, N_KV_A)[None]
    const = lambda shape: pl.BlockSpec(shape, lambda i, j: (0,) * len(shape))
    tok = lambda w: pl.BlockSpec((None, tm, w), lambda i, j: (j, i, 0))
    out_shapes = (
        jax.ShapeDtypeStruct((b, s, N_HEADS_A * LANES), BF16),
        jax.ShapeDtypeStruct((b, s, KA_W), BF16),
        jax.ShapeDtypeStruct((b, KA_W, s), BF16),
        jax.ShapeDtypeStruct((b, s, QB_W), BF16),
        jax.ShapeDtypeStruct((b, s, QB_W), BF16),
        jax.ShapeDtypeStruct((b, s, QB_W), BF16),
    )
    return pl.pallas_call(
        _inproj_kernel,
        grid=(s // tm, b),
        in_specs=[
            tok(d),
            const((1, d)),
            const((d, D_IN)),
            const((KA_W, d)),
            pl.BlockSpec((tm, QA_W), lambda i, j: (i, 0)),
            pl.BlockSpec((tm, QA_W), lambda i, j: (i, 0)),
            const((1, QA_W)),
            const((1, KA_W)),
            const((QA_W, QA_W)),
            const((KA_W, KA_W)),
        ],
        out_specs=(
            tok(N_HEADS_A * LANES),
            tok(KA_W),
            pl.BlockSpec((None, KA_W, tm), lambda i, j: (j, 0, i)),
            tok(QB_W), tok(QB_W), tok(QB_W),
        ),
        out_shape=out_shapes,
        compiler_params=_cparams("arbitrary", "arbitrary"),
        name="inproj",
    )(x, attn_norm[None], w_in_bf, w_in_bf[:, QA_W + KA_W:QA_W + 2 * KA_W].T, cos, sin, qg, kg, _blockdiag(QA_W), _blockdiag(KA_W))


def _gattn_kernel(q_ref, k_ref, vt_ref, o_ref, s0_ref, s1_ref, *, kv_tile):
    tq = q_ref.shape[0]
    group = N_HEADS_A // N_KV_A
    qb = q_ref[...]
    q4 = jnp.concatenate([qb[:, LANES * i:LANES * (i + 1)] for i in range(group)], axis=0)
    n_q = group * tq
    n_kv = k_ref.shape[0] // kv_tile
    ones = jnp.ones((16, kv_tile), BF16)

    def scores(j, dst):
        off = pl.multiple_of(j * kv_tile, kv_tile)
        dst[...] = lax.dot_general(k_ref[pl.ds(off, kv_tile), :], q4, NT_DIMS,
                                   preferred_element_type=F32)

    def consume(j, src, m, acc):
        off = pl.multiple_of(j * kv_tile, kv_tile)
        s = src[...]
        m_new = jnp.maximum(m, jnp.max(s, axis=0, keepdims=True))
        p = jnp.exp2(s - m_new).astype(BF16)
        va = jnp.concatenate([vt_ref[:, pl.ds(off, kv_tile)], ones], axis=0)
        return m_new, acc * jnp.exp2(m - m_new) + jnp.dot(va, p, preferred_element_type=F32)

    def body(i, carry):
        m, acc = carry
        scores(2 * i + 1, s1_ref)
        m, acc = consume(2 * i, s0_ref, m, acc)
        scores(2 * i + 2, s0_ref)
        return consume(2 * i + 1, s1_ref, m, acc)

    scores(0, s0_ref)
    init = (jnp.full((1, n_q), NEG, F32), jnp.zeros((HEAD_DIM + 16, n_q), F32))
    m, acc = lax.fori_loop(0, n_kv // 2 - 1, body, init)
    scores(n_kv - 1, s1_ref)
    m, acc = consume(n_kv - 2, s0_ref, m, acc)
    m, acc = consume(n_kv - 1, s1_ref, m, acc)
    ot = acc[:HEAD_DIM] / acc[HEAD_DIM:HEAD_DIM + 1]
    stacked = jnp.concatenate([ot[:, tq * i:tq * (i + 1)] for i in range(group)], axis=0)
    o_ref[...] = stacked.T.astype(o_ref.dtype)


def _gattn(qa, ka, vt, tq, kv_tile):
    b, s, _ = ka.shape
    group = N_HEADS_A // N_KV_A
    assert s % (2 * kv_tile) == 0 and s >= 2 * kv_tile
    return pl.pallas_call(
        functools.partial(_gattn_kernel, kv_tile=kv_tile),
        grid=(b, N_KV_A, s // tq),
        in_specs=[
            pl.BlockSpec((None, tq, group * LANES), lambda bi, g, i: (bi, i, g)),
            pl.BlockSpec((None, s, KA_W), lambda bi, g, i: (bi, 0, 0)),
            pl.BlockSpec((None, HEAD_DIM, s), lambda bi, g, i: (bi, g, 0)),
        ],
        out_specs=pl.BlockSpec((None, tq, group * HEAD_DIM), lambda bi, g, i: (bi, i, g)),
        out_shape=jax.ShapeDtypeStruct((b, s, QA_W), BF16),
        scratch_shapes=[pltpu.VMEM((kv_tile, group * tq), F32)] * 2,
        compiler_params=_cparams("arbitrary", "arbitrary", "arbitrary"),
        name="gattn",
    )(qa, ka, vt)


NA_STEP_ROWS = 4


def _nattn_kernel(q_ref, k_ref, v_ref, tbl_ref, o_ref, *, rows):
    win = NB_ROWS * GRID_W
    lane = lax.broadcasted_iota(jnp.int32, (GRID_W, LANES), 1)
    low = lane < HEAD_DIM
    for a in range(NA_STEP_ROWS):
        r = pl.program_id(1) * NA_STEP_ROWS + a
        rs = jnp.clip(r - NB_ROWS // 2, 0, rows - NB_ROWS)
        off = pl.multiple_of(rs * GRID_W, GRID_W)
        qrows = slice(GRID_W * a, GRID_W * (a + 1))
        for pp in range(N_HEADS_B // 2):
            cols = slice(LANES * pp, LANES * (pp + 1))
            qp = q_ref[qrows, cols]
            zero = jnp.zeros_like(qp)
            q2 = jnp.concatenate([jnp.where(low, qp, zero), jnp.where(low, zero, qp)], axis=0)
            bias = jnp.concatenate([tbl_ref[a, 2 * pp], tbl_ref[a, 2 * pp + 1]], axis=0)
            s = lax.dot_general(q2, k_ref[pl.ds(off, win), cols], NT_DIMS, preferred_element_type=F32) + bias
            p = jnp.exp(s - jnp.max(s, axis=-1, keepdims=True))
            l = jnp.sum(p, axis=-1, keepdims=True)
            o2 = jnp.dot(p.astype(BF16), v_ref[pl.ds(off, win), cols], preferred_element_type=F32) / l
            o_ref[qrows, cols] = jnp.where(low, o2[:GRID_W], o2[GRID_W:]).astype(o_ref.dtype)


def _na_bias_table(rpb, rows):
    qc = jnp.arange(GRID_W)
    kc = jnp.arange(GRID_W)
    cstart = jnp.clip(qc - NB_COLS // 2, 0, GRID_W - NB_COLS)
    valid = (kc[None, :] >= cstart[:, None]) & (kc[None, :] < cstart[:, None] + NB_COLS)
    pad = GRID_W - NB_COLS
    rp = jnp.pad(rpb.astype(F32), ((0, 0), (0, 0), (pad, pad)))
    toe = jnp.stack([rp[:, :, GRID_W - 1 - q:2 * GRID_W - 1 - q] for q in range(GRID_W)], axis=2)
    toe = jnp.where(valid[None, None], toe, NEG)

    def for_class(c):
        t = toe[:, NB_ROWS - 1 - c:2 * NB_ROWS - 1 - c]
        return t.transpose(0, 2, 1, 3).reshape(N_HEADS_B, GRID_W, NB_ROWS * GRID_W)

    def cls_of(r):
        return r - min(max(r - NB_ROWS // 2, 0), rows - NB_ROWS)

    kinds = []
    for first_row in (0, NA_STEP_ROWS, rows - NA_STEP_ROWS):
        kinds.append(jnp.stack([for_class(cls_of(first_row + a)) for a in range(NA_STEP_ROWS)]))
    return jnp.stack(kinds)


def _nattn(qb, kb, vb, rpb):
    b, s, w = qb.shape
    rows = s // GRID_W
    n_steps = rows // NA_STEP_ROWS
    assert rows % NA_STEP_ROWS == 0 and n_steps >= 3 and NA_STEP_ROWS >= NB_ROWS // 2
    tbl = _na_bias_table(rpb, rows)
    tq = NA_STEP_ROWS * GRID_W

    def kind_map(bi, i):
        return (jnp.where(i == 0, 0, jnp.where(i == n_steps - 1, 2, 1)), 0, 0, 0, 0)

    return pl.pallas_call(
        functools.partial(_nattn_kernel, rows=rows),
        grid=(b, n_steps),
        in_specs=[
            pl.BlockSpec((None, tq, w), lambda bi, i: (bi, i, 0)),
            pl.BlockSpec((None, s, w), lambda bi, i: (bi, 0, 0)),
            pl.BlockSpec((None, s, w), lambda bi, i: (bi, 0, 0)),
            pl.BlockSpec((None, NA_STEP_ROWS, N_HEADS_B, GRID_W, NB_ROWS * GRID_W), kind_map),
        ],
        out_specs=pl.BlockSpec((None, tq, w), lambda bi, i: (bi, i, 0)),
        out_shape=jax.ShapeDtypeStruct((b, s, w), BF16),
        compiler_params=_cparams("arbitrary", "arbitrary"),
        name="nattn",
    )(qb, kb, vb, tbl)


def _memkv_kernel(mem_ref, g_ref, w_ref, k_ref, v_ref):
    m = _rms(mem_ref[...], g_ref[...]).astype(BF16)
    k_ref[...] = jnp.dot(m, w_ref[:, :D_MODEL], preferred_element_type=F32).astype(BF16)
    v_ref[...] = jnp.dot(m, w_ref[:, D_MODEL:], preferred_element_type=F32).astype(BF16)


def _memkv(mem, mem_norm, w_mkv_bf):
    b, m, d = mem.shape
    blk = pl.BlockSpec((None, m, d), lambda i: (i, 0, 0))
    return pl.pallas_call(
        _memkv_kernel,
        grid=(b,),
        in_specs=[blk, pl.BlockSpec((1, d), lambda i: (0, 0)), pl.BlockSpec((d, 2 * d), lambda i: (0, 0))],
        out_specs=(blk, blk),
        out_shape=(jax.ShapeDtypeStruct((b, m, d), BF16),) * 2,
        compiler_params=_cparams("arbitrary"),
        name="memkv",
    )(mem, mem_norm[None], w_mkv_bf)


def _mix_kernel(x_ref, ya_ref, yb_ref, mk_ref, mv_ref, ga_ref, gb_ref, wo_ref, gx_ref, wq_ref, wmo_ref,
                gf_ref, wr_hl_ref, wr_h_ref, x2_ref, h3_ref, aff_ref):
    na = _rms(ya_ref[...].astype(F32), ga_ref[...]).astype(BF16)
    nb = _rms(yb_ref[...].astype(F32), gb_ref[...]).astype(BF16)
    x1 = (x_ref[...]
          + jnp.dot(na, wo_ref[:QA_W, :], preferred_element_type=F32)
          + jnp.dot(nb, wo_ref[QA_W:, :], preferred_element_type=F32))

    h2 = _rms(x1, gx_ref[...]).astype(BF16)
    q = (jnp.dot(h2, wq_ref[...], preferred_element_type=F32) * (MEM_HEAD_DIM ** -0.5)).astype(BF16)
    heads = []
    for hh in range(MEM_HEADS):
        cols = slice(MEM_HEAD_DIM * hh, MEM_HEAD_DIM * (hh + 1))
        s = lax.dot_general(q[:, cols], mk_ref[:, cols], NT_DIMS, preferred_element_type=F32)
        m = jnp.max(s, axis=-1, keepdims=True)
        p = jnp.exp(s - m)
        l = jnp.sum(p, axis=-1, keepdims=True)
        heads.append((jnp.dot(p.astype(BF16), mv_ref[:, cols], preferred_element_type=F32) / l).astype(BF16))
    o = jnp.concatenate(heads, axis=1)
    x2 = x1 + jnp.dot(o, wmo_ref[...], preferred_element_type=F32)
    x2_ref[...] = x2

    h3 = _rms(x2, gf_ref[...])
    h3_ref[...] = h3
    hi, lo = _split_bf16(h3)
    r1 = lax.dot_general(wr_hl_ref[...], hi, NT_DIMS, preferred_element_type=F32)
    r2 = lax.dot_general(wr_h_ref[...], lo, NT_DIMS, preferred_element_type=F32)
    logits = r1[:N_EXPERTS] + r1[N_EXPERTS:] + r2
    e = jnp.exp(logits - jnp.max(logits, axis=0, keepdims=True))
    aff_ref[...] = e / jnp.sum(e, axis=0, keepdims=True)


def _mix(x, ya, yb, mk, mv, out_norm_a, out_norm_b, w_out_bf, xattn_norm, w_mq_bf, w_mo_bf, ffn_norm,
         w_router, tm):
    b, s, d = x.shape
    n = b * s
    tiles_per_seq = s // tm
    wr_t = w_router.T
    wr_hi, wr_lo = _split_bf16(wr_t)
    wr_hl = jnp.concatenate([wr_hi, wr_lo], axis=0)
    const = lambda shape: pl.BlockSpec(shape, lambda i: (0,) * len(shape))
    tok = lambda w: pl.BlockSpec((tm, w), lambda i: (i, 0))
    memb = pl.BlockSpec((None, N_MEM, d), lambda i: (i // tiles_per_seq, 0, 0))
    return pl.pallas_call(
        _mix_kernel,
        grid=(n // tm,),
        in_specs=[
            tok(d), tok(QA_W), tok(QB_W), memb, memb,
            const((1, QA_W)), const((1, QB_W)), const((d, d)),
            const((1, d)), const((d, d)), const((d, d)),
            const((1, d)), const((2 * N_EXPERTS, d)), const((N_EXPERTS, d)),
        ],
        out_specs=(tok(d), tok(d), pl.BlockSpec((N_EXPERTS, tm), lambda i: (0, i))),
        out_shape=(
            jax.ShapeDtypeStruct((n, d), F32),
            jax.ShapeDtypeStruct((n, d), F32),
            jax.ShapeDtypeStruct((N_EXPERTS, n), F32),
        ),
        compiler_params=_cparams("arbitrary"),
        name="mix",
    )(x.reshape(n, d), ya.reshape(n, QA_W), yb.reshape(n, QB_W), mk, mv, out_norm_a[None], out_norm_b[None],
      w_out_bf, xattn_norm[None], w_mq_bf, w_mo_bf, ffn_norm[None], wr_hl, wr_hi)


def _front(x, mem, p, *, tm_in=512, tq=128, kv_tile=512, tm_mix=256):
    qa, ka, vt, qb, kb, vb = _inproj(x, p["attn_norm"], p["w_in"], p["q_norm"], p["k_norm"], tm_in)
    ya = _gattn(qa, ka, vt, tq, kv_tile)
    yb = _nattn(qb, kb, vb, p["rpb"])
    mk, mv = _memkv(mem, p["mem_norm"], p["w_mkv"])
    return _mix(x, ya, yb, mk, mv, p["out_norm_a"], p["out_norm_b"], p["w_out"], p["xattn_norm"],
                p["w_mq"], p["w_mo"], p["ffn_norm"], p["w_router"], tm_mix)


def _tri(n, strict_lower=False):
    i = lax.broadcasted_iota(jnp.int32, (n, n), 0)
    j = lax.broadcasted_iota(jnp.int32, (n, n), 1)
    return ((j < i) if strict_lower else (i <= j)).astype(BF16)


def _select_kernel(aff_ref, slot_ref, sel_ref, off_ref, *, cap):
    n_e, rows, _ = aff_ref.shape

    def bits_of(e):
        return lax.bitcast_convert_type(aff_ref[e], jnp.int32)

    def step(i, prefixes):
        bit = jnp.left_shift(jnp.int32(1), 30 - i)
        out = []
        for e in range(n_e):
            cand = prefixes[e] | bit
            cnt = jnp.sum(jnp.where(bits_of(e) >= cand, 1.0, 0.0), axis=(0, 1), keepdims=True)
            out.append(jnp.where(cnt >= float(cap), cand, prefixes[e]))
        return tuple(out)

    thr = lax.fori_loop(0, 31, step, tuple(jnp.zeros((1, 1), jnp.int32) for _ in range(n_e)))

    u_lane = _tri(LANES)
    u_rows = _tri(rows)
    l_rows = _tri(rows, strict_lower=True)
    ones_l = jnp.ones((LANES, LANES), BF16)
    ones_8 = jnp.ones((8, LANES), BF16)

    def prefix(mask_bf):
        within = jnp.dot(mask_bf, u_lane, preferred_element_type=F32)
        row_tot = jnp.dot(mask_bf, ones_l, preferred_element_type=F32)
        before = jnp.dot(l_rows, row_tot.astype(BF16), preferred_element_type=F32)
        return within, before

    for e in range(n_e):
        b = bits_of(e)
        gt = b > thr[e]
        eq = b == thr[e]
        need = float(cap) - jnp.sum(jnp.where(gt, 1.0, 0.0), axis=(0, 1), keepdims=True)
        eq_f = jnp.where(eq, 1.0, 0.0)
        within, before = prefix(eq_f.astype(BF16))
        sel = gt | (eq & ((before + within - eq_f) < need))
        sel_f = jnp.where(sel, 1.0, 0.0)
        sel_bf = sel_f.astype(BF16)
        within, before = prefix(sel_bf)
        rank = (before + within - 1.0).astype(jnp.int32)
        slot_ref[e] = jnp.where(sel, rank + e * cap, -1)
        sel_ref[e] = sel_bf
        cnt_row = lax.dot_general(ones_8, sel_bf, NT_DIMS, preferred_element_type=F32)
        end_row = jnp.dot(cnt_row.astype(BF16), u_rows, preferred_element_type=F32)
        off_ref[e] = end_row - cnt_row


def _select(aff3, cap):
    n_e, rows, _ = aff3.shape
    return pl.pallas_call(
        functools.partial(_select_kernel, cap=cap),
        out_shape=(
            jax.ShapeDtypeStruct((n_e, rows, LANES), jnp.int32),
            jax.ShapeDtypeStruct((n_e, rows, LANES), BF16),
            jax.ShapeDtypeStruct((n_e, 8, rows), F32),
        ),
        compiler_params=pltpu.CompilerParams(vmem_limit_bytes=VMEM_LIMIT),
        name="select",
    )(aff3)


def _compact_kernel(sel_ref, aff_ref, idx_ref, gate_ref, *, jt):
    rows = sel_ref.shape[0]
    sel = sel_ref[...]
    ones_8 = jnp.ones((8, LANES), BF16)
    cnt_row = lax.dot_general(ones_8, sel, NT_DIMS, preferred_element_type=F32)
    end_row = jnp.dot(cnt_row.astype(BF16), _tri(rows), preferred_element_type=F32)
    row_end = end_row[0:1, :]
    row_off = row_end - cnt_row[0:1, :]

    j0 = pl.program_id(1) * jt
    j_r = (j0 + lax.broadcasted_iota(jnp.int32, (jt, rows), 0)).astype(F32)
    below = jnp.where(row_end <= j_r, 1.0, 0.0).astype(BF16)
    hit = jnp.where((row_end > j_r) & (row_off <= j_r), 1.0, 0.0).astype(BF16)

    cnt_b = jnp.dot(sel, jnp.ones((LANES, LANES), BF16), preferred_element_type=F32)
    off_b = jnp.dot(below, cnt_b.astype(BF16), preferred_element_type=F32)
    sel_row = jnp.dot(hit, sel, preferred_element_type=F32)
    within = jnp.dot(sel_row.astype(BF16), _tri(LANES), preferred_element_type=F32)
    j_l = (j0 + lax.broadcasted_iota(jnp.int32, (jt, LANES), 0)).astype(F32)
    pick = (sel_row > 0.5) & (within == (j_l - off_b + 1.0))
    pick_bf = jnp.where(pick, 1.0, 0.0).astype(BF16)

    sub = lax.broadcasted_iota(jnp.int32, (8, LANES), 0)
    lane_id = jnp.where(sub == 0, lax.broadcasted_iota(jnp.int32, (8, LANES), 1), 0).astype(BF16)
    lane_pos = lax.dot_general(lane_id, pick_bf, NT_DIMS, preferred_element_type=F32)
    sub_r = lax.broadcasted_iota(jnp.int32, (8, rows), 0)
    r_id = lax.broadcasted_iota(jnp.int32, (8, rows), 1)
    r_split = jnp.where(sub_r == 0, r_id // 16, jnp.where(sub_r == 1, r_id % 16, 0)).astype(BF16)
    row_pos = lax.dot_general(r_split, hit, NT_DIMS, preferred_element_type=F32)
    tok = (row_pos[0:1, :] * 16.0 + row_pos[1:2, :]) * float(LANES) + lane_pos[0:1, :]
    idx_ref[...] = tok.astype(jnp.int32)

    a = aff_ref[...]
    a0 = a.astype(BF16)
    a1 = (a - a0.astype(F32)).astype(BF16)
    a2 = (a - a0.astype(F32) - a1.astype(F32)).astype(BF16)
    gate = jnp.zeros((1, jt), F32)
    for piece in (a0, a1, a2):
        g = jnp.dot(hit, piece, preferred_element_type=F32)
        g = jnp.where(pick, g, 0.0).astype(BF16)
        gate = gate + lax.dot_general(ones_8, g, NT_DIMS, preferred_element_type=F32)[0:1, :]
    gate_ref[...] = gate


def _compact(sel, aff3, cap, jt):
    n_e, rows, _ = sel.shape
    blk = pl.BlockSpec((None, rows, LANES), lambda e, j: (e, 0, 0))
    out_blk = pl.BlockSpec((None, 1, jt), lambda e, j: (e, 0, j))
    return pl.pallas_call(
        functools.partial(_compact_kernel, jt=jt),
        grid=(n_e, cap // jt),
        in_specs=[blk, blk],
        out_specs=(out_blk, out_blk),
        out_shape=(jax.ShapeDtypeStruct((n_e, 1, cap), jnp.int32), jax.ShapeDtypeStruct((n_e, 1, cap), F32)),
        compiler_params=_cparams("arbitrary", "arbitrary"),
        name="compact",
    )(sel, aff3)


def _ffn_kernel(idx_ref, nxt_ref, gate_ref, h_hbm, wg_ref, wu_ref, wd_ref, o_ref, xbuf, sems, *, fc):
    ts = xbuf.shape[1]
    g = pl.program_id(0)
    cur = g % 2

    def row_copy(half, j, src_row):
        return pltpu.make_async_copy(h_hbm.at[pl.ds(src_row, 1), :], xbuf.at[half, pl.ds(j, 1), :],
                                     sems.at[half])

    def gather(ids_ref, half):
        def issue(j, c):
            row_copy(half, j, ids_ref[0, j]).start()
            return c
        lax.fori_loop(0, ts, issue, 0, unroll=8)

    @pl.when(g == 0)
    def _():
        gather(idx_ref, cur)

    @pl.when(g + 1 < pl.num_programs(0))
    def _():
        gather(nxt_ref, 1 - cur)

    def drain(j, c):
        row_copy(cur, j, 0).wait()
        return c

    lax.fori_loop(0, ts, drain, 0, unroll=8)

    xe = xbuf[cur].astype(BF16)
    acc = jnp.zeros((ts, D_MODEL), F32)
    for c in range(D_FF // fc):
        cols = slice(fc * c, fc * (c + 1))
        g = jnp.dot(xe, wg_ref[:, cols], preferred_element_type=F32)
        u = jnp.dot(xe, wu_ref[:, cols], preferred_element_type=F32)
        he = (g / (1.0 + jnp.exp(-g)) * u).astype(BF16)
        acc = acc + jnp.dot(he, wd_ref[cols, :], preferred_element_type=F32)
    gate_col = jnp.broadcast_to(gate_ref[...], (LANES, ts)).T[:, 0:1]
    o_ref[...] = (acc * gate_col).astype(o_ref.dtype)


def _ffn(idx, gate, h3, wg, wu, wd, ts, fc):
    n_e, _, cap = idx.shape
    d = h3.shape[-1]
    nt = cap // ts
    n_steps = n_e * nt
    idx3 = idx.reshape(n_steps, 1, ts)
    ids = lambda shift: pl.BlockSpec((None, 1, ts), lambda g: (jnp.minimum(g + shift, n_steps - 1), 0, 0),
                                     memory_space=pltpu.SMEM)
    wspec = lambda r, c: pl.BlockSpec((None, r, c), lambda g: (g // nt, 0, 0))
    return pl.pallas_call(
        functools.partial(_ffn_kernel, fc=fc),
        grid=(n_steps,),
        in_specs=[
            ids(0), ids(1),
            pl.BlockSpec((None, 1, ts), lambda g: (g // nt, 0, g % nt)),
            pl.BlockSpec(memory_space=pl.ANY),
            wspec(d, D_FF), wspec(d, D_FF), wspec(D_FF, d),
        ],
        out_specs=pl.BlockSpec((ts, d), lambda g: (g, 0)),
        out_shape=jax.ShapeDtypeStruct((n_e * cap, d), BF16),
        scratch_shapes=[pltpu.VMEM((2, ts, d), F32), pltpu.SemaphoreType.DMA((2,))],
        compiler_params=_cparams("arbitrary"),
        name="ffn",
    )(idx3, idx3, gate, h3, wg, wu, wd)


COMBINE_WIN = 64


def _combine_kernel(base_ref, rounds_ref, x_ref, slot_ref, ye_hbm, g_ref, o_ref, buf, sems):
    n_e, t = slot_ref.shape
    total = ye_hbm.shape[0]
    tt = pl.program_id(0)
    cur = tt % 2
    w_iota = lax.broadcasted_iota(jnp.int32, (COMBINE_WIN, t), 0)

    def windows(tile, r):
        lows = [base_ref[e, tile] + r * COMBINE_WIN for e in range(n_e)]
        return lows, [pl.multiple_of(jnp.minimum(lo, total - COMBINE_WIN), 16) for lo in lows]

    def win_copy(half, e, start):
        return pltpu.make_async_copy(ye_hbm.at[pl.ds(start, COMBINE_WIN), :],
                                     buf.at[half, pl.ds(e * COMBINE_WIN, COMBINE_WIN), :], sems.at[half])

    def fetch(tile, r, half):
        _, starts = windows(tile, r)
        for e in range(n_e):
            win_copy(half, e, starts[e]).start()

    def scatter_add(r, acc):
        lows, starts = windows(tt, r)
        for e in range(n_e):
            win_copy(cur, e, starts[e]).wait()
        onehot_t = jnp.concatenate(
            [jnp.where((slot_ref[e:e + 1, :] == starts[e] + w_iota) & (slot_ref[e:e + 1, :] >= lows[e]),
                       1.0, 0.0).astype(BF16) for e in range(n_e)],
            axis=0)
        return acc + lax.dot_general(onehot_t, buf[cur], TN_DIMS, preferred_element_type=F32)

    @pl.when(tt == 0)
    def _():
        fetch(tt, 0, cur)

    @pl.when(tt + 1 < pl.num_programs(0))
    def _():
        fetch(tt + 1, 0, 1 - cur)

    y = scatter_add(0, jnp.zeros((t, D_MODEL), F32))

    def later_round(r, acc):
        fetch(tt, r, cur)
        return scatter_add(r, acc)

    y = lax.fori_loop(1, rounds_ref[tt], later_round, y)
    o_ref[...] = _rms(x_ref[...] + y, g_ref[...])


def _combine(x2, slot, ye, base, rounds, final_norm, t):
    n, d = x2.shape
    n_e = slot.shape[0]
    return pl.pallas_call(
        _combine_kernel,
        grid_spec=pltpu.PrefetchScalarGridSpec(
            num_scalar_prefetch=2,
            grid=(n // t,),
            in_specs=[
                pl.BlockSpec((t, d), lambda i, *_: (i, 0)),
                pl.BlockSpec((n_e, t), lambda i, *_: (0, i)),
                pl.BlockSpec(memory_space=pl.ANY),
                pl.BlockSpec((1, d), lambda i, *_: (0, 0)),
            ],
            out_specs=pl.BlockSpec((t, d), lambda i, *_: (i, 0)),
            scratch_shapes=[pltpu.VMEM((2, n_e * COMBINE_WIN, d), BF16), pltpu.SemaphoreType.DMA((2,))],
        ),
        out_shape=jax.ShapeDtypeStruct((n, d), F32),
        compiler_params=_cparams("arbitrary"),
        name="combine",
    )(base, rounds, x2, slot, ye, final_norm[None])


def _moe(x2, h3, aff, p, *, jt=512, ts=512, fc=512, t=256):
    n, d = x2.shape
    n_e = aff.shape[0]
    cap = EC_FACTOR * n // n_e
    rows = n // LANES
    aff3 = aff.reshape(n_e, rows, LANES)
    slot, sel, off = _select(aff3, cap)
    idx, gate = _compact(sel, aff3, cap, min(jt, cap))
    ye = _ffn(idx, gate, h3, p["w_gate"], p["w_up"], p["w_down"], min(ts, cap), fc)

    rows_per_tile = t // LANES
    first = off[:, 0, ::rows_per_tile].astype(jnp.int32) + cap * jnp.arange(n_e, dtype=jnp.int32)[:, None]
    nxt = jnp.concatenate([first[:, 1:], (cap * (jnp.arange(n_e, dtype=jnp.int32) + 1))[:, None]], axis=1)
    base = (first // 16) * 16
    rounds = jnp.max((nxt - base + COMBINE_WIN - 1) // COMBINE_WIN, axis=0).astype(jnp.int32)
    rounds = jnp.where(jnp.max(nxt - first, axis=0) > 0, rounds, 0)
    return _combine(x2, slot.reshape(n_e, n), ye, base, rounds, p["final_norm"], t)


def kernel(x_prompt, x_sample, mem_prompt, mem_sample, attn_norm, w_in, q_norm, k_norm, rpb, out_norm_a,
           out_norm_b, w_out, xattn_norm, mem_norm, w_mq, w_mkv, w_mo, ffn_norm, w_router, w_gate, w_up,
           w_down, final_norm):
    assert attn_norm.shape[0] == 1, "single-layer encoder"
    p = dict(attn_norm=attn_norm[0], w_in=w_in[0].astype(BF16), q_norm=q_norm[0], k_norm=k_norm[0],
             rpb=rpb[0], out_norm_a=out_norm_a[0], out_norm_b=out_norm_b[0], w_out=w_out[0].astype(BF16),
             xattn_norm=xattn_norm[0], mem_norm=mem_norm[0], w_mq=w_mq[0].astype(BF16),
             w_mkv=w_mkv[0].astype(BF16), w_mo=w_mo[0].astype(BF16), ffn_norm=ffn_norm[0],
             w_router=w_router[0], w_gate=w_gate[0].astype(BF16), w_up=w_up[0].astype(BF16),
             w_down=w_down[0].astype(BF16), final_norm=final_norm)
    outs = []
    for x, mem in ((x_prompt, mem_prompt), (x_sample, mem_sample)):
        x2, h3, aff = _front(x, mem, p)
        outs.append(_moe(x2, h3, aff, p).reshape(x.shape))
    return tuple(outs)
```

```python
import functools

import jax
import jax.numpy as jnp
from jax import lax
from jax.experimental import pallas as pl
from jax.experimental.pallas import tpu as pltpu

F32 = jnp.float32
BF16 = jnp.bfloat16

D_MODEL = 1024
GRID_W = 64
HEAD_DIM = 64
N_HEADS_A = 8
N_KV_A = 2
N_HEADS_B = 8
QA_W = N_HEADS_A * HEAD_DIM
KA_W = N_KV_A * HEAD_DIM
QB_W = N_HEADS_B * HEAD_DIM
D_IN = QA_W + 2 * KA_W + 3 * QB_W
ROPE_THETA = 10000.0
NB_ROWS = 8
NB_COLS = 16
N_MEM = 256
MEM_HEADS = 4
MEM_HEAD_DIM = D_MODEL // MEM_HEADS
N_EXPERTS = 16
EC_FACTOR = 2
D_FF = 2 * D_MODEL
EPS = 1e-6
NEG = -1e30
LOG2E = 1.4426950408889634

LANES = 128
VMEM_LIMIT = 56 * 1024 * 1024

NT_DIMS = (((1,), (1,)), ((), ()))
TN_DIMS = (((0,), (0,)), ((), ()))


def _cparams(*sem):
    return pltpu.CompilerParams(dimension_semantics=sem, vmem_limit_bytes=VMEM_LIMIT)


def _rms(x, g):
    return x * lax.rsqrt(jnp.mean(x * x, axis=-1, keepdims=True) + EPS) * g


def _split_bf16(x):
    hi = x.astype(BF16)
    lo = (x - hi.astype(F32)).astype(BF16)
    return hi, lo


def _head_norm_rope(z, gain, cos, sin_signed, blockdiag):
    w = z.shape[-1]
    hi, lo = _split_bf16(z * z)
    ss = (jnp.dot(hi, blockdiag, preferred_element_type=F32)
          + jnp.dot(lo, blockdiag, preferred_element_type=F32))
    y = z * lax.rsqrt(ss * (1.0 / HEAD_DIM) + EPS) * gain
    lane = lax.broadcasted_iota(jnp.int32, y.shape, 1)
    first = (lane % 32) < 16
    rot = jnp.where(first, pltpu.roll(y, w - 16, axis=1), pltpu.roll(y, 16, axis=1))
    return y * cos + rot * sin_signed


def _inproj_kernel(x_ref, g_ref, w_ref, wvt_ref, cos_ref, sin_ref, qg_ref, kg_ref, bdq_ref, bdk_ref,
                   qa_ref, ka_ref, vt_ref, qb_ref, kb_ref, vb_ref):
    h = _rms(x_ref[...], g_ref[...]).astype(BF16)

    def proj(lo, hi):
        return jnp.dot(h, w_ref[:, lo:hi], preferred_element_type=F32)

    cos = cos_ref[...]
    sin = sin_ref[...]
    q = _head_norm_rope(proj(0, QA_W), qg_ref[...], cos, sin, bdq_ref[...])
    k = _head_norm_rope(proj(QA_W, QA_W + KA_W), kg_ref[...], cos[:, :KA_W], sin[:, :KA_W],
                        bdk_ref[...])
    vt = lax.dot_general(wvt_ref[...], h, NT_DIMS, preferred_element_type=F32)

    lane = lax.broadcasted_iota(jnp.int32, (q.shape[0], LANES), 1)
    group = N_HEADS_A // N_KV_A
    for i in range(N_HEADS_A):
        src = q[:, LANES * (i // 2):LANES * (i // 2 + 1)]
        kv = i // group
        if (i % 2) != kv:
            src = pltpu.roll(src, HEAD_DIM, axis=1)
        keep = (lane >= HEAD_DIM * kv) & (lane < HEAD_DIM * (kv + 1))
        qa_ref[:, LANES * i:LANES * (i + 1)] = jnp.where(keep, src, 0.0).astype(BF16)
    ka_ref[...] = k.astype(BF16)
    vt_ref[...] = vt.astype(BF16)

    o = QA_W + 2 * KA_W
    qb_ref[...] = (proj(o, o + QB_W) * (HEAD_DIM ** -0.5)).astype(BF16)
    kb_ref[...] = proj(o + QB_W, o + 2 * QB_W).astype(BF16)
    vb_ref[...] = proj(o + 2 * QB_W, o + 3 * QB_W).astype(BF16)


def _rope_tables(seq_len):
    half = HEAD_DIM // 2
    inv = ROPE_THETA ** (-jnp.arange(0, half, 2, dtype=F32) / half)
    t = jnp.arange(seq_len)
    row = (t // GRID_W).astype(F32)
    col = (t % GRID_W).astype(F32)
    ar = row[:, None] * inv[None]
    ac = col[:, None] * inv[None]
    ang = jnp.concatenate([ar, ar, ac, ac], axis=-1)
    sign = jnp.where((jnp.arange(HEAD_DIM) % 32) < 16, -1.0, 1.0).astype(F32)
    cos = jnp.tile(jnp.cos(ang), (1, N_HEADS_A))
    sin = jnp.tile(jnp.sin(ang) * sign[None], (1, N_HEADS_A))
    return cos, sin


def _blockdiag(width):
    i = jnp.arange(width) // HEAD_DIM
    return (i[:, None] == i[None, :]).astype(BF16)


def _inproj(x, attn_norm, w_in_bf, q_norm, k_norm, tm):
    b, s, d = x.shape
    cos, sin = _rope_tables(s)
    qg = (jnp.tile(q_norm, N_HEADS_A) * (HEAD_DIM ** -0.5 * LOG2E))[None]
    kg = jnp.tile(k_norm, N_KV_A)[None]
    const = lambda shape: pl.BlockSpec(shape, lambda i, j: (0,) * len(shape))
    tok = lambda w: pl.BlockSpec((None, tm, w), lambda i, j: (j, i, 0))
    out_shapes = (
        jax.ShapeDtypeStruct((b, s, N_HEADS_A * LANES), BF16),
        jax.ShapeDtypeStruct((b, s, KA_W), BF16),
        jax.ShapeDtypeStruct((b, KA_W, s), BF16),
        jax.ShapeDtypeStruct((b, s, QB_W), BF16),
        jax.ShapeDtypeStruct((b, s, QB_W), BF16),
        jax.ShapeDtypeStruct((b, s, QB_W), BF16),
    )
    return pl.pallas_call(
        _inproj_kernel,
        grid=(s // tm, b),
        in_specs=[
            tok(d),
            const((1, d)),
            const((d, D_IN)),
            const((KA_W, d)),
            pl.BlockSpec((tm, QA_W), lambda i, j: (i, 0)),
            pl.BlockSpec((tm, QA_W), lambda i, j: (i, 0)),
            const((1, QA_W)),
            const((1, KA_W)),
            const((QA_W, QA_W)),
            const((KA_W, KA_W)),
        ],
        out_specs=(
            tok(N_HEADS_A * LANES),
            tok(KA_W),
            pl.BlockSpec((None, KA_W, tm), lambda i, j: (j, 0, i)),
            tok(QB_W), tok(QB_W), tok(QB_W),
        ),
        out_shape=out_shapes,
        compiler_params=_cparams("arbitrary", "arbitrary"),
        name="inproj",
    )(x, attn_norm[None], w_in_bf, w_in_bf[:, QA_W + KA_W:QA_W + 2 * KA_W].T, cos, sin, qg, kg, _blockdiag(QA_W), _blockdiag(KA_W))


def _gattn_kernel(q_ref, k_ref, vt_ref, o_ref, s0_ref, s1_ref, *, kv_tile):
    tq = q_ref.shape[0]
    group = N_HEADS_A // N_KV_A
    qb = q_ref[...]
    q4 = jnp.concatenate([qb[:, LANES * i:LANES * (i + 1)] for i in range(group)], axis=0)
    n_q = group * tq
    n_kv = k_ref.shape[0] // kv_tile
    ones = jnp.ones((16, kv_tile), BF16)

    def scores(j, dst):
        off = pl.multiple_of(j * kv_tile, kv_tile)
        dst[...] = lax.dot_general(k_ref[pl.ds(off, kv_tile), :], q4, NT_DIMS,
                                   preferred_element_type=F32)

    def consume(j, src, m, acc):
        off = pl.multiple_of(j * kv_tile, kv_tile)
        s = src[...]
        m_new = jnp.maximum(m, jnp.max(s, axis=0, keepdims=True))
        p = jnp.exp2(s - m_new).astype(BF16)
        va = jnp.concatenate([vt_ref[:, pl.ds(off, kv_tile)], ones], axis=0)
        return m_new, acc * jnp.exp2(m - m_new) + jnp.dot(va, p, preferred_element_type=F32)

    def body(i, carry):
        m, acc = carry
        scores(2 * i + 1, s1_ref)
        m, acc = consume(2 * i, s0_ref, m, acc)
        scores(2 * i + 2, s0_ref)
        return consume(2 * i + 1, s1_ref, m, acc)

    scores(0, s0_ref)
    init = (jnp.full((1, n_q), NEG, F32), jnp.zeros((HEAD_DIM + 16, n_q), F32))
    m, acc = lax.fori_loop(0, n_kv // 2 - 1, body, init)
    scores(n_kv - 1, s1_ref)
    m, acc = consume(n_kv - 2, s0_ref, m, acc)
    m, acc = consume(n_kv - 1, s1_ref, m, acc)
    ot = acc[:HEAD_DIM] / acc[HEAD_DIM:HEAD_DIM + 1]
    stacked = jnp.concatenate([ot[:, tq * i:tq * (i + 1)] for i in range(group)], axis=0)
    o_ref[...] = stacked.T.astype(o_ref.dtype)


def _gattn(qa, ka, vt, tq, kv_tile):
    b, s, _ = ka.shape
    group = N_HEADS_A // N_KV_A
    assert s % (2 * kv_tile) == 0 and s >= 2 * kv_tile
    return pl.pallas_call(
        functools.partial(_gattn_kernel, kv_tile=kv_tile),
        grid=(b, N_KV_A, s // tq),
        in_specs=[
            pl.BlockSpec((None, tq, group * LANES), lambda bi, g, i: (bi, i, g)),
            pl.BlockSpec((None, s, KA_W), lambda bi, g, i: (bi, 0, 0)),
            pl.BlockSpec((None, HEAD_DIM, s), lambda bi, g, i: (bi, g, 0)),
        ],
        out_specs=pl.BlockSpec((None, tq, group * HEAD_DIM), lambda bi, g, i: (bi, i, g)),
        out_shape=jax.ShapeDtypeStruct((b, s, QA_W), BF16),
        scratch_shapes=[pltpu.VMEM((kv_tile, group * tq), F32)] * 2,
        compiler_params=_cparams("arbitrary", "arbitrary", "arbitrary"),
        name="gattn",
    )(qa, ka, vt)


NA_STEP_ROWS = 4
NA_WIN_ROWS = NA_STEP_ROWS + NB_ROWS


def _nattn_kernel(q_ref, k_ref, v_ref, tbl_ref, o_ref, *, rows):
    i = pl.program_id(1)
    ws = jnp.clip(NA_STEP_ROWS * i - NB_ROWS // 2, 0, rows - NA_WIN_ROWS)
    off = pl.multiple_of(ws * GRID_W, GRID_W)
    win = NA_WIN_ROWS * GRID_W
    tq = NA_STEP_ROWS * GRID_W
    low = lax.broadcasted_iota(jnp.int32, (tq, LANES), 1) < HEAD_DIM
    for pp in range(N_HEADS_B // 2):
        cols = slice(LANES * pp, LANES * (pp + 1))
        qp = q_ref[:, cols]
        zero = jnp.zeros_like(qp)
        q2 = jnp.concatenate([jnp.where(low, qp, zero), jnp.where(low, zero, qp)], axis=0)
        s = lax.dot_general(q2, k_ref[pl.ds(off, win), cols], NT_DIMS, preferred_element_type=F32)
        s = s + tbl_ref[pp]
        p = jnp.exp(s - jnp.max(s, axis=-1, keepdims=True))
        l = jnp.sum(p, axis=-1, keepdims=True)
        o2 = jnp.dot(p.astype(BF16), v_ref[pl.ds(off, win), cols], preferred_element_type=F32) / l
        o_ref[:, cols] = jnp.where(low, o2[:tq], o2[tq:]).astype(o_ref.dtype)


def _na_bias_table(rpb):
    qc = jnp.arange(GRID_W)
    kc = jnp.arange(GRID_W)
    cstart = jnp.clip(qc - NB_COLS // 2, 0, GRID_W - NB_COLS)
    valid = (kc[None, :] >= cstart[:, None]) & (kc[None, :] < cstart[:, None] + NB_COLS)
    pad = GRID_W - NB_COLS
    rp = jnp.pad(rpb.astype(F32), ((0, 0), (0, 0), (pad, pad)))
    toe = jnp.stack([rp[:, :, GRID_W - 1 - q:2 * GRID_W - 1 - q] for q in range(GRID_W)], axis=2)
    toe = jnp.where(valid[None, None], toe, NEG)

    def rows_for(rel, lo):
        d0 = lo - rel + NB_ROWS - 1
        own = toe[:, d0:d0 + NB_ROWS]
        neg = lambda n: jnp.full((N_HEADS_B, n, GRID_W, GRID_W), NEG, F32)
        full = jnp.concatenate([neg(lo), own, neg(NA_WIN_ROWS - NB_ROWS - lo)], axis=1)
        return full.transpose(0, 2, 1, 3).reshape(N_HEADS_B, GRID_W, NA_WIN_ROWS * GRID_W)

    kinds = []
    for rel0, lo_of in ((0, lambda a: 0), (NB_ROWS // 2, lambda a: a), (NB_ROWS, lambda a: NB_ROWS // 2)):
        t = jnp.stack([rows_for(rel0 + a, lo_of(a)) for a in range(NA_STEP_ROWS)], axis=1)
        kinds.append(t.reshape(N_HEADS_B // 2, 2 * NA_STEP_ROWS * GRID_W, NA_WIN_ROWS * GRID_W))
    return jnp.stack(kinds)


def _nattn(qb, kb, vb, rpb):
    b, s, w = qb.shape
    rows = s // GRID_W
    n_steps = rows // NA_STEP_ROWS
    assert rows % NA_STEP_ROWS == 0 and n_steps >= 3 and NA_STEP_ROWS == NB_ROWS // 2
    tbl = _na_bias_table(rpb)
    tq = NA_STEP_ROWS * GRID_W

    def kind_map(bi, i):
        return (jnp.where(i == 0, 0, jnp.where(i == n_steps - 1, 2, 1)), 0, 0, 0)

    return pl.pallas_call(
        functools.partial(_nattn_kernel, rows=rows),
        grid=(b, n_steps),
        in_specs=[
            pl.BlockSpec((None, tq, w), lambda bi, i: (bi, i, 0)),
            pl.BlockSpec((None, s, w), lambda bi, i: (bi, 0, 0)),
            pl.BlockSpec((None, s, w), lambda bi, i: (bi, 0, 0)),
            pl.BlockSpec((None, N_HEADS_B // 2, 2 * tq, NA_WIN_ROWS * GRID_W), kind_map),
        ],
        out_specs=pl.BlockSpec((None, tq, w), lambda bi, i: (bi, i, 0)),
        out_shape=jax.ShapeDtypeStruct((b, s, w), BF16),
        compiler_params=_cparams("arbitrary", "arbitrary"),
        name="nattn",
    )(qb, kb, vb, tbl)


def _memkv_kernel(mem_ref, g_ref, w_ref, k_ref, v_ref):
    m = _rms(mem_ref[...], g_ref[...]).astype(BF16)
    k_ref[...] = jnp.dot(m, w_ref[:, :D_MODEL], preferred_element_type=F32).astype(BF16)
    v_ref[...] = jnp.dot(m, w_ref[:, D_MODEL:], preferred_element_type=F32).astype(BF16)


def _memkv(mem, mem_norm, w_mkv_bf):
    b, m, d = mem.shape
    blk = pl.BlockSpec((None, m, d), lambda i: (i, 0, 0))
    return pl.pallas_call(
        _memkv_kernel,
        grid=(b,),
        in_specs=[blk, pl.BlockSpec((1, d), lambda i: (0, 0)), pl.BlockSpec((d, 2 * d), lambda i: (0, 0))],
        out_specs=(blk, blk),
        out_shape=(jax.ShapeDtypeStruct((b, m, d), BF16),) * 2,
        compiler_params=_cparams("arbitrary"),
        name="memkv",
    )(mem, mem_norm[None], w_mkv_bf)


def _mix_kernel(x_ref, ya_ref, yb_ref, mk_ref, mv_ref, ga_ref, gb_ref, wo_ref, gx_ref, wq_ref, wmo_ref,
                gf_ref, wr_hl_ref, wr_h_ref, x2_ref, h3_ref, aff_ref):
    na = _rms(ya_ref[...].astype(F32), ga_ref[...]).astype(BF16)
    nb = _rms(yb_ref[...].astype(F32), gb_ref[...]).astype(BF16)
    x1 = (x_ref[...]
          + jnp.dot(na, wo_ref[:QA_W, :], preferred_element_type=F32)
          + jnp.dot(nb, wo_ref[QA_W:, :], preferred_element_type=F32))

    h2 = _rms(x1, gx_ref[...]).astype(BF16)
    q = (jnp.dot(h2, wq_ref[...], preferred_element_type=F32) * (MEM_HEAD_DIM ** -0.5)).astype(BF16)
    heads = []
    for hh in range(MEM_HEADS):
        cols = slice(MEM_HEAD_DIM * hh, MEM_HEAD_DIM * (hh + 1))
        s = lax.dot_general(q[:, cols], mk_ref[:, cols], NT_DIMS, preferred_element_type=F32)
        m = jnp.max(s, axis=-1, keepdims=True)
        p = jnp.exp(s - m)
        l = jnp.sum(p, axis=-1, keepdims=True)
        heads.append((jnp.dot(p.astype(BF16), mv_ref[:, cols], preferred_element_type=F32) / l).astype(BF16))
    o = jnp.concatenate(heads, axis=1)
    x2 = x1 + jnp.dot(o, wmo_ref[...], preferred_element_type=F32)
    x2_ref[...] = x2

    h3 = _rms(x2, gf_ref[...])
    h3_ref[...] = h3
    hi, lo = _split_bf16(h3)
    r1 = lax.dot_general(wr_hl_ref[...], hi, NT_DIMS, preferred_element_type=F32)
    r2 = lax.dot_general(wr_h_ref[...], lo, NT_DIMS, preferred_element_type=F32)
    logits = r1[:N_EXPERTS] + r1[N_EXPERTS:] + r2
    e = jnp.exp(logits - jnp.max(logits, axis=0, keepdims=True))
    aff_ref[...] = e / jnp.sum(e, axis=0, keepdims=True)


def _mix(x, ya, yb, mk, mv, out_norm_a, out_norm_b, w_out_bf, xattn_norm, w_mq_bf, w_mo_bf, ffn_norm,
         w_router, tm):
    b, s, d = x.shape
    n = b * s
    tiles_per_seq = s // tm
    wr_t = w_router.T
    wr_hi, wr_lo = _split_bf16(wr_t)
    wr_hl = jnp.concatenate([wr_hi, wr_lo], axis=0)
    const = lambda shape: pl.BlockSpec(shape, lambda i: (0,) * len(shape))
    tok = lambda w: pl.BlockSpec((tm, w), lambda i: (i, 0))
    memb = pl.BlockSpec((None, N_MEM, d), lambda i: (i // tiles_per_seq, 0, 0))
    return pl.pallas_call(
        _mix_kernel,
        grid=(n // tm,),
        in_specs=[
            tok(d), tok(QA_W), tok(QB_W), memb, memb,
            const((1, QA_W)), const((1, QB_W)), const((d, d)),
            const((1, d)), const((d, d)), const((d, d)),
            const((1, d)), const((2 * N_EXPERTS, d)), const((N_EXPERTS, d)),
        ],
        out_specs=(tok(d), tok(d), pl.BlockSpec((N_EXPERTS, tm), lambda i: (0, i))),
        out_shape=(
            jax.ShapeDtypeStruct((n, d), F32),
            jax.ShapeDtypeStruct((n, d), F32),
            jax.ShapeDtypeStruct((N_EXPERTS, n), F32),
        ),
        compiler_params=_cparams("arbitrary"),
        name="mix",
    )(x.reshape(n, d), ya.reshape(n, QA_W), yb.reshape(n, QB_W), mk, mv, out_norm_a[None], out_norm_b[None],
      w_out_bf, xattn_norm[None], w_mq_bf, w_mo_bf, ffn_norm[None], wr_hl, wr_hi)


def _front(x, mem, p, *, tm_in=512, tq=128, kv_tile=1024, tm_mix=512):
    qa, ka, vt, qb, kb, vb = _inproj(x, p["attn_norm"], p["w_in"], p["q_norm"], p["k_norm"], tm_in)
    ya = _gattn(qa, ka, vt, tq, min(kv_tile, x.shape[1] // 2))
    yb = _nattn(qb, kb, vb, p["rpb"])
    mk, mv = _memkv(mem, p["mem_norm"], p["w_mkv"])
    return _mix(x, ya, yb, mk, mv, p["out_norm_a"], p["out_norm_b"], p["w_out"], p["xattn_norm"],
                p["w_mq"], p["w_mo"], p["ffn_norm"], p["w_router"], tm_mix)


def _tri(n, strict_lower=False):
    i = lax.broadcasted_iota(jnp.int32, (n, n), 0)
    j = lax.broadcasted_iota(jnp.int32, (n, n), 1)
    return ((j < i) if strict_lower else (i <= j)).astype(BF16)


def _select_kernel(aff_ref, slot_ref, sel_ref, off_ref, *, cap):
    n_e, rows, _ = aff_ref.shape

    def bits_of(e):
        return lax.bitcast_convert_type(aff_ref[e], jnp.int32)

    def step(i, prefixes):
        bit = jnp.left_shift(jnp.int32(1), 30 - i)
        out = []
        for e in range(n_e):
            cand = prefixes[e] | bit
            cnt = jnp.sum(jnp.where(bits_of(e) >= cand, 1.0, 0.0), axis=(0, 1), keepdims=True)
            out.append(jnp.where(cnt >= float(cap), cand, prefixes[e]))
        return tuple(out)

    thr = lax.fori_loop(0, 31, step, tuple(jnp.zeros((1, 1), jnp.int32) for _ in range(n_e)))

    u_lane = _tri(LANES)
    u_rows = _tri(rows)
    l_rows = _tri(rows, strict_lower=True)
    ones_l = jnp.ones((LANES, LANES), BF16)
    ones_8 = jnp.ones((8, LANES), BF16)

    def prefix(mask_bf):
        within = jnp.dot(mask_bf, u_lane, preferred_element_type=F32)
        row_tot = jnp.dot(mask_bf, ones_l, preferred_element_type=F32)
        before = jnp.dot(l_rows, row_tot.astype(BF16), preferred_element_type=F32)
        return within, before

    for e in range(n_e):
        b = bits_of(e)
        gt = b > thr[e]
        eq = b == thr[e]
        need = float(cap) - jnp.sum(jnp.where(gt, 1.0, 0.0), axis=(0, 1), keepdims=True)
        eq_f = jnp.where(eq, 1.0, 0.0)
        within, before = prefix(eq_f.astype(BF16))
        sel = gt | (eq & ((before + within - eq_f) < need))
        sel_f = jnp.where(sel, 1.0, 0.0)
        sel_bf = sel_f.astype(BF16)
        within, before = prefix(sel_bf)
        rank = (before + within - 1.0).astype(jnp.int32)
        slot_ref[e] = jnp.where(sel, rank + e * cap, -1)
        sel_ref[e] = sel_bf
        cnt_row = lax.dot_general(ones_8, sel_bf, NT_DIMS, preferred_element_type=F32)
        end_row = jnp.dot(cnt_row.astype(BF16), u_rows, preferred_element_type=F32)
        off_ref[e] = end_row - cnt_row


def _select(aff3, cap):
    n_e, rows, _ = aff3.shape
    return pl.pallas_call(
        functools.partial(_select_kernel, cap=cap),
        out_shape=(
            jax.ShapeDtypeStruct((n_e, rows, LANES), jnp.int32),
            jax.ShapeDtypeStruct((n_e, rows, LANES), BF16),
            jax.ShapeDtypeStruct((n_e, 8, rows), F32),
        ),
        compiler_params=pltpu.CompilerParams(vmem_limit_bytes=VMEM_LIMIT),
        name="select",
    )(aff3)


def _compact_kernel(sel_ref, aff_ref, idx_ref, gate_ref, *, jt):
    rows = sel_ref.shape[0]
    sel = sel_ref[...]
    ones_8 = jnp.ones((8, LANES), BF16)
    cnt_row = lax.dot_general(ones_8, sel, NT_DIMS, preferred_element_type=F32)
    end_row = jnp.dot(cnt_row.astype(BF16), _tri(rows), preferred_element_type=F32)
    row_end = end_row[0:1, :]
    row_off = row_end - cnt_row[0:1, :]

    j0 = pl.program_id(1) * jt
    j_r = (j0 + lax.broadcasted_iota(jnp.int32, (jt, rows), 0)).astype(F32)
    below = jnp.where(row_end <= j_r, 1.0, 0.0).astype(BF16)
    hit = jnp.where((row_end > j_r) & (row_off <= j_r), 1.0, 0.0).astype(BF16)

    cnt_b = jnp.dot(sel, jnp.ones((LANES, LANES), BF16), preferred_element_type=F32)
    off_b = jnp.dot(below, cnt_b.astype(BF16), preferred_element_type=F32)
    sel_row = jnp.dot(hit, sel, preferred_element_type=F32)
    within = jnp.dot(sel_row.astype(BF16), _tri(LANES), preferred_element_type=F32)
    j_l = (j0 + lax.broadcasted_iota(jnp.int32, (jt, LANES), 0)).astype(F32)
    pick = (sel_row > 0.5) & (within == (j_l - off_b + 1.0))
    pick_bf = jnp.where(pick, 1.0, 0.0).astype(BF16)

    sub = lax.broadcasted_iota(jnp.int32, (8, LANES), 0)
    lane_id = jnp.where(sub == 0, lax.broadcasted_iota(jnp.int32, (8, LANES), 1), 0).astype(BF16)
    lane_pos = lax.dot_general(lane_id, pick_bf, NT_DIMS, preferred_element_type=F32)
    sub_r = lax.broadcasted_iota(jnp.int32, (8, rows), 0)
    r_id = lax.broadcasted_iota(jnp.int32, (8, rows), 1)
    r_split = jnp.where(sub_r == 0, r_id // 16, jnp.where(sub_r == 1, r_id % 16, 0)).astype(BF16)
    row_pos = lax.dot_general(r_split, hit, NT_DIMS, preferred_element_type=F32)
    tok = (row_pos[0:1, :] * 16.0 + row_pos[1:2, :]) * float(LANES) + lane_pos[0:1, :]
    idx_ref[...] = tok.astype(jnp.int32)

    a = aff_ref[...]
    a0 = a.astype(BF16)
    a1 = (a - a0.astype(F32)).astype(BF16)
    a2 = (a - a0.astype(F32) - a1.astype(F32)).astype(BF16)
    gate = jnp.zeros((1, jt), F32)
    for piece in (a0, a1, a2):
        g = jnp.dot(hit, piece, preferred_element_type=F32)
        g = jnp.where(pick, g, 0.0).astype(BF16)
        gate = gate + lax.dot_general(ones_8, g, NT_DIMS, preferred_element_type=F32)[0:1, :]
    gate_ref[...] = gate


def _compact(sel, aff3, cap, jt):
    n_e, rows, _ = sel.shape
    blk = pl.BlockSpec((None, rows, LANES), lambda e, j: (e, 0, 0))
    out_blk = pl.BlockSpec((None, 1, jt), lambda e, j: (e, 0, j))
    return pl.pallas_call(
        functools.partial(_compact_kernel, jt=jt),
        grid=(n_e, cap // jt),
        in_specs=[blk, blk],
        out_specs=(out_blk, out_blk),
        out_shape=(jax.ShapeDtypeStruct((n_e, 1, cap), jnp.int32), jax.ShapeDtypeStruct((n_e, 1, cap), F32)),
        compiler_params=_cparams("arbitrary", "arbitrary"),
        name="compact",
    )(sel, aff3)


def _ffn_kernel(idx_ref, nxt_ref, gate_ref, h_hbm, wg_ref, wu_ref, wd_ref, o_ref, xbuf, sems, *, fc):
    ts = xbuf.shape[1]
    g = pl.program_id(0)
    cur = g % 2

    def row_copy(half, j, src_row):
        return pltpu.make_async_copy(h_hbm.at[pl.ds(src_row, 1), :], xbuf.at[half, pl.ds(j, 1), :],
                                     sems.at[half])

    def gather(ids_ref, half):
        for j in range(ts):
            row_copy(half, j, ids_ref[0, j]).start()

    @pl.when(g == 0)
    def _():
        gather(idx_ref, 0)

    has_next = g + 1 < pl.num_programs(0)
    for half in range(2):
        @pl.when(has_next & (cur != half))
        def _(half=half):
            gather(nxt_ref, half)

    def drain(j, c):
        row_copy(cur, j, 0).wait()
        return c

    lax.fori_loop(0, ts, drain, 0, unroll=8)

    xe = xbuf[cur].astype(BF16)
    acc = jnp.zeros((ts, D_MODEL), F32)
    for c in range(D_FF // fc):
        cols = slice(fc * c, fc * (c + 1))
        g = jnp.dot(xe, wg_ref[:, cols], preferred_element_type=F32)
        u = jnp.dot(xe, wu_ref[:, cols], preferred_element_type=F32)
        he = (g / (1.0 + jnp.exp(-g)) * u).astype(BF16)
        acc = acc + jnp.dot(he, wd_ref[cols, :], preferred_element_type=F32)
    gate_col = jnp.broadcast_to(gate_ref[...], (LANES, ts)).T[:, 0:1]
    o_ref[...] = (acc * gate_col).astype(o_ref.dtype)


def _ffn(idx, gate, h3, wg, wu, wd, ts, fc):
    n_e, _, cap = idx.shape
    d = h3.shape[-1]
    nt = cap // ts
    n_steps = n_e * nt
    idx3 = idx.reshape(n_steps, 1, ts)
    ids = lambda shift: pl.BlockSpec((None, 1, ts), lambda g: (jnp.minimum(g + shift, n_steps - 1), 0, 0),
                                     memory_space=pltpu.SMEM)
    wspec = lambda r, c: pl.BlockSpec((None, r, c), lambda g: (g // nt, 0, 0))
    return pl.pallas_call(
        functools.partial(_ffn_kernel, fc=fc),
        grid=(n_steps,),
        in_specs=[
            ids(0), ids(1),
            pl.BlockSpec((None, 1, ts), lambda g: (g // nt, 0, g % nt)),
            pl.BlockSpec(memory_space=pl.ANY),
            wspec(d, D_FF), wspec(d, D_FF), wspec(D_FF, d),
        ],
        out_specs=pl.BlockSpec((ts, d), lambda g: (g, 0)),
        out_shape=jax.ShapeDtypeStruct((n_e * cap, d), BF16),
        scratch_shapes=[pltpu.VMEM((2, ts, d), F32), pltpu.SemaphoreType.DMA((2,))],
        compiler_params=_cparams("arbitrary"),
        name="ffn",
    )(idx3, idx3, gate, h3, wg, wu, wd)


COMBINE_WIN = 64


def _combine_kernel(base_ref, rounds_ref, x_ref, slot_ref, ye_hbm, g_ref, o_ref, buf, sems):
    n_e, t = slot_ref.shape
    total = ye_hbm.shape[0]
    tt = pl.program_id(0)
    cur = tt % 2
    w_iota = lax.broadcasted_iota(jnp.int32, (COMBINE_WIN, t), 0)

    def windows(tile, r):
        lows = [base_ref[e, tile] + r * COMBINE_WIN for e in range(n_e)]
        return lows, [pl.multiple_of(jnp.minimum(lo, total - COMBINE_WIN), 16) for lo in lows]

    def win_copy(half, e, start):
        return pltpu.make_async_copy(ye_hbm.at[pl.ds(start, COMBINE_WIN), :],
                                     buf.at[half, pl.ds(e * COMBINE_WIN, COMBINE_WIN), :], sems.at[half])

    def fetch(tile, r, half):
        _, starts = windows(tile, r)
        for e in range(n_e):
            win_copy(half, e, starts[e]).start()

    def scatter_add(r, acc):
        lows, starts = windows(tt, r)
        for e in range(n_e):
            win_copy(cur, e, starts[e]).wait()
        onehot_t = jnp.concatenate(
            [jnp.where((slot_ref[e:e + 1, :] == starts[e] + w_iota) & (slot_ref[e:e + 1, :] >= lows[e]),
                       1.0, 0.0).astype(BF16) for e in range(n_e)],
            axis=0)
        return acc + lax.dot_general(onehot_t, buf[cur], TN_DIMS, preferred_element_type=F32)

    @pl.when(tt == 0)
    def _():
        fetch(tt, 0, cur)

    @pl.when(tt + 1 < pl.num_programs(0))
    def _():
        fetch(tt + 1, 0, 1 - cur)

    y = scatter_add(0, jnp.zeros((t, D_MODEL), F32))

    def later_round(r, acc):
        fetch(tt, r, cur)
        return scatter_add(r, acc)

    y = lax.fori_loop(1, rounds_ref[tt], later_round, y)
    o_ref[...] = _rms(x_ref[...] + y, g_ref[...])


def _combine(x2, slot, ye, base, rounds, final_norm, t):
    n, d = x2.shape
    n_e = slot.shape[0]
    return pl.pallas_call(
        _combine_kernel,
        grid_spec=pltpu.PrefetchScalarGridSpec(
            num_scalar_prefetch=2,
            grid=(n // t,),
            in_specs=[
                pl.BlockSpec((t, d), lambda i, *_: (i, 0)),
                pl.BlockSpec((n_e, t), lambda i, *_: (0, i)),
                pl.BlockSpec(memory_space=pl.ANY),
                pl.BlockSpec((1, d), lambda i, *_: (0, 0)),
            ],
            out_specs=pl.BlockSpec((t, d), lambda i, *_: (i, 0)),
            scratch_shapes=[pltpu.VMEM((2, n_e * COMBINE_WIN, d), BF16), pltpu.SemaphoreType.DMA((2,))],
        ),
        out_shape=jax.ShapeDtypeStruct((n, d), F32),
        compiler_params=_cparams("arbitrary"),
        name="combine",
    )(base, rounds, x2, slot, ye, final_norm[None])


def _moe(x2, h3, aff, p, *, jt=512, ts=512, fc=512, t=256):
    n, d = x2.shape
    n_e = aff.shape[0]
    cap = EC_FACTOR * n // n_e
    rows = n // LANES
    aff3 = aff.reshape(n_e, rows, LANES)
    slot, sel, off = _select(aff3, cap)
    idx, gate = _compact(sel, aff3, cap, min(jt, cap))
    ye = _ffn(idx, gate, h3, p["w_gate"], p["w_up"], p["w_down"], min(ts, cap), fc)

    rows_per_tile = t // LANES
    first = off[:, 0, ::rows_per_tile].astype(jnp.int32) + cap * jnp.arange(n_e, dtype=jnp.int32)[:, None]
    nxt = jnp.concatenate([first[:, 1:], (cap * (jnp.arange(n_e, dtype=jnp.int32) + 1))[:, None]], axis=1)
    base = (first // 16) * 16
    rounds = jnp.max((nxt - base + COMBINE_WIN - 1) // COMBINE_WIN, axis=0).astype(jnp.int32)
    rounds = jnp.where(jnp.max(nxt - first, axis=0) > 0, rounds, 0)
    return _combine(x2, slot.reshape(n_e, n), ye, base, rounds, p["final_norm"], t)


def kernel(x_prompt, x_sample, mem_prompt, mem_sample, attn_norm, w_in, q_norm, k_norm, rpb, out_norm_a,
           out_norm_b, w_out, xattn_norm, mem_norm, w_mq, w_mkv, w_mo, ffn_norm, w_router, w_gate, w_up,
           w_down, final_norm):
    assert attn_norm.shape[0] == 1, "single-layer encoder"
    p = dict(attn_norm=attn_norm[0], w_in=w_in[0].astype(BF16), q_norm=q_norm[0], k_norm=k_norm[0],
             rpb=rpb[0], out_norm_a=out_norm_a[0], out_norm_b=out_norm_b[0], w_out=w_out[0].astype(BF16),
             xattn_norm=xattn_norm[0], mem_norm=mem_norm[0], w_mq=w_mq[0].astype(BF16),
             w_mkv=w_mkv[0].astype(BF16), w_mo=w_mo[0].astype(BF16), ffn_norm=ffn_norm[0],
             w_router=w_router[0], w_gate=w_gate[0].astype(BF16), w_up=w_up[0].astype(BF16),
             w_down=w_down[0].astype(BF16), final_norm=final_norm)
    outs = []
    for x, mem in ((x_prompt, mem_prompt), (x_sample, mem_sample)):
        x2, h3, aff = _front(x, mem, p)
        outs.append(_moe(x2, h3, aff, p).reshape(x.shape))
    return tuple(outs)
```

```python
import functools

import jax
import jax.numpy as jnp
from jax import lax
from jax.experimental import pallas as pl
from jax.experimental.pallas import tpu as pltpu

F32 = jnp.float32
BF16 = jnp.bfloat16

D_MODEL = 1024
GRID_W = 64
HEAD_DIM = 64
N_HEADS_A = 8
N_KV_A = 2
N_HEADS_B = 8
QA_W = N_HEADS_A * HEAD_DIM
KA_W = N_KV_A * HEAD_DIM
QB_W = N_HEADS_B * HEAD_DIM
D_IN = QA_W + 2 * KA_W + 3 * QB_W
ROPE_THETA = 10000.0
NB_ROWS = 8
NB_COLS = 16
N_MEM = 256
MEM_HEADS = 4
MEM_HEAD_DIM = D_MODEL // MEM_HEADS
N_EXPERTS = 16
EC_FACTOR = 2
D_FF = 2 * D_MODEL
EPS = 1e-6
NEG = -1e30
LOG2E = 1.4426950408889634

LANES = 128
VMEM_LIMIT = 56 * 1024 * 1024

NT_DIMS = (((1,), (1,)), ((), ()))
TN_DIMS = (((0,), (0,)), ((), ()))


def _cparams(*sem):
    return pltpu.CompilerParams(dimension_semantics=sem, vmem_limit_bytes=VMEM_LIMIT)


def _rms(x, g):
    return x * lax.rsqrt(jnp.mean(x * x, axis=-1, keepdims=True) + EPS) * g


def _split_bf16(x):
    hi = x.astype(BF16)
    lo = (x - hi.astype(F32)).astype(BF16)
    return hi, lo


def _head_norm_rope(z, gain, cos, sin_signed, blockdiag):
    w = z.shape[-1]
    hi, lo = _split_bf16(z * z)
    ss = (jnp.dot(hi, blockdiag, preferred_element_type=F32)
          + jnp.dot(lo, blockdiag, preferred_element_type=F32))
    y = z * lax.rsqrt(ss * (1.0 / HEAD_DIM) + EPS) * gain
    lane = lax.broadcasted_iota(jnp.int32, y.shape, 1)
    first = (lane % 32) < 16
    rot = jnp.where(first, pltpu.roll(y, w - 16, axis=1), pltpu.roll(y, 16, axis=1))
    return y * cos + rot * sin_signed


def _inproj_kernel(x_ref, g_ref, w_ref, wvt_ref, cos_ref, sin_ref, qg_ref, kg_ref, bdq_ref, bdk_ref,
                   qa_ref, ka_ref, vt_ref, qb_ref, kb_ref, vb_ref):
    h = _rms(x_ref[...], g_ref[...]).astype(BF16)

    def proj(lo, hi):
        return jnp.dot(h, w_ref[:, lo:hi], preferred_element_type=F32)

    cos = cos_ref[...]
    sin = sin_ref[...]
    q = _head_norm_rope(proj(0, QA_W), qg_ref[...], cos, sin, bdq_ref[...])
    k = _head_norm_rope(proj(QA_W, QA_W + KA_W), kg_ref[...], cos[:, :KA_W], sin[:, :KA_W],
                        bdk_ref[...])
    vt = lax.dot_general(wvt_ref[...], h, NT_DIMS, preferred_element_type=F32)

    lane = lax.broadcasted_iota(jnp.int32, (q.shape[0], LANES), 1)
    group = N_HEADS_A // N_KV_A
    for i in range(N_HEADS_A):
        src = q[:, LANES * (i // 2):LANES * (i // 2 + 1)]
        kv = i // group
        if (i % 2) != kv:
            src = pltpu.roll(src, HEAD_DIM, axis=1)
        keep = (lane >= HEAD_DIM * kv) & (lane < HEAD_DIM * (kv + 1))
        qa_ref[:, LANES * i:LANES * (i + 1)] = jnp.where(keep, src, 0.0).astype(BF16)
    ka_ref[...] = k.astype(BF16)
    vt_ref[...] = vt.astype(BF16)

    o = QA_W + 2 * KA_W
    qb_ref[...] = (proj(o, o + QB_W) * (HEAD_DIM ** -0.5)).astype(BF16)
    kb_ref[...] = proj(o + QB_W, o + 2 * QB_W).astype(BF16)
    vb_ref[...] = proj(o + 2 * QB_W, o + 3 * QB_W).astype(BF16)


def _rope_tables(seq_len):
    half = HEAD_DIM // 2
    inv = ROPE_THETA ** (-jnp.arange(0, half, 2, dtype=F32) / half)
    t = jnp.arange(seq_len)
    row = (t // GRID_W).astype(F32)
    col = (t % GRID_W).astype(F32)
    ar = row[:, None] * inv[None]
    ac = col[:, None] * inv[None]
    ang = jnp.concatenate([ar, ar, ac, ac], axis=-1)
    sign = jnp.where((jnp.arange(HEAD_DIM) % 32) < 16, -1.0, 1.0).astype(F32)
    cos = jnp.tile(jnp.cos(ang), (1, N_HEADS_A))
    sin = jnp.tile(jnp.sin(ang) * sign[None], (1, N_HEADS_A))
    return cos, sin


def _blockdiag(width):
    i = jnp.arange(width) // HEAD_DIM
    return (i[:, None] == i[None, :]).astype(BF16)


def _inproj(x, attn_norm, w_in_bf, q_norm, k_norm, tm):
    b, s, d = x.shape
    cos, sin = _rope_tables(s)
    qg = (jnp.tile(q_norm, N_HEADS_A) * (HEAD_DIM ** -0.5 * LOG2E))[None]
    kg = jnp.tile(k_norm, N_KV_A)[None]
    const = lambda shape: pl.BlockSpec(shape, lambda i, j: (0,) * len(shape))
    tok = lambda w: pl.BlockSpec((None, tm, w), lambda i, j: (j, i, 0))
    out_shapes = (
        jax.ShapeDtypeStruct((b, s, N_HEADS_A * LANES), BF16),
        jax.ShapeDtypeStruct((b, s, KA_W), BF16),
        jax.ShapeDtypeStruct((b, KA_W, s), BF16),
        jax.ShapeDtypeStruct((b, s, QB_W), BF16),
        jax.ShapeDtypeStruct((b, s, QB_W), BF16),
        jax.ShapeDtypeStruct((b, s, QB_W), BF16),
    )
    return pl.pallas_call(
        _inproj_kernel,
        grid=(s // tm, b),
        in_specs=[
            tok(d),
            const((1, d)),
            const((d, D_IN)),
            const((KA_W, d)),
            pl.BlockSpec((tm, QA_W), lambda i, j: (i, 0)),
            pl.BlockSpec((tm, QA_W), lambda i, j: (i, 0)),
            const((1, QA_W)),
            const((1, KA_W)),
            const((QA_W, QA_W)),
            const((KA_W, KA_W)),
        ],
        out_specs=(
            tok(N_HEADS_A * LANES),
            tok(KA_W),
            pl.BlockSpec((None, KA_W, tm), lambda i, j: (j, 0, i)),
            tok(QB_W), tok(QB_W), tok(QB_W),
        ),
        out_shape=out_shapes,
        compiler_params=_cparams("arbitrary", "arbitrary"),
        name="inproj",
    )(x, attn_norm[None], w_in_bf, w_in_bf[:, QA_W + KA_W:QA_W + 2 * KA_W].T, cos, sin, qg, kg, _blockdiag(QA_W), _blockdiag(KA_W))


def _gattn_kernel(q_ref, qn_ref, k_ref, kn_ref, vt_ref, o_ref, s0_ref, s1_ref, *, kv_tile):
    tq = q_ref.shape[0]
    group = N_HEADS_A // N_KV_A
    stack_heads = lambda qb: jnp.concatenate([qb[:, LANES * i:LANES * (i + 1)] for i in range(group)], axis=0)
    q4 = stack_heads(q_ref[...])
    n_q = group * tq
    n_kv = k_ref.shape[0] // kv_tile
    ones = jnp.ones((16, kv_tile), BF16)

    def scores(j, dst, keys=k_ref, queries=q4):
        off = pl.multiple_of(j * kv_tile, kv_tile)
        dst[...] = lax.dot_general(keys[pl.ds(off, kv_tile), :], queries, NT_DIMS,
                                   preferred_element_type=F32)

    def consume(j, src, m, acc):
        off = pl.multiple_of(j * kv_tile, kv_tile)
        s = src[...]
        m_new = jnp.maximum(m, jnp.max(s, axis=0, keepdims=True))
        p = jnp.exp2(s - m_new).astype(BF16)
        va = jnp.concatenate([vt_ref[:, pl.ds(off, kv_tile)], ones], axis=0)
        return m_new, acc * jnp.exp2(m - m_new) + jnp.dot(va, p, preferred_element_type=F32)

    def body(i, carry):
        m, acc = carry
        scores(2 * i + 1, s1_ref)
        m, acc = consume(2 * i, s0_ref, m, acc)
        scores(2 * i + 2, s0_ref)
        return consume(2 * i + 1, s1_ref, m, acc)

    @pl.when((pl.program_id(0) == 0) & (pl.program_id(1) == 0) & (pl.program_id(2) == 0))
    def _():
        scores(0, s0_ref)

    init = (jnp.full((1, n_q), NEG, F32), jnp.zeros((HEAD_DIM + 16, n_q), F32))
    m, acc = lax.fori_loop(0, n_kv // 2 - 1, body, init)
    scores(n_kv - 1, s1_ref)
    m, acc = consume(n_kv - 2, s0_ref, m, acc)
    scores(0, s0_ref, kn_ref, stack_heads(qn_ref[...]))
    m, acc = consume(n_kv - 1, s1_ref, m, acc)
    ot = acc[:HEAD_DIM] / acc[HEAD_DIM:HEAD_DIM + 1]
    stacked = jnp.concatenate([ot[:, tq * i:tq * (i + 1)] for i in range(group)], axis=0)
    o_ref[...] = stacked.T.astype(o_ref.dtype)


def _gattn(qa, ka, vt, tq, kv_tile):
    b, s, _ = ka.shape
    group = N_HEADS_A // N_KV_A
    assert s % (2 * kv_tile) == 0 and s >= 2 * kv_tile
    n_i = s // tq
    n_steps = b * N_KV_A * n_i

    def next_step(bi, g, i):
        flat = jnp.minimum((bi * N_KV_A + g) * n_i + i + 1, n_steps - 1)
        return flat // (N_KV_A * n_i), (flat // n_i) % N_KV_A, flat % n_i

    def q_next(bi, g, i):
        nb, ng, ni = next_step(bi, g, i)
        return nb, ni, ng

    return pl.pallas_call(
        functools.partial(_gattn_kernel, kv_tile=kv_tile),
        grid=(b, N_KV_A, n_i),
        in_specs=[
            pl.BlockSpec((None, tq, group * LANES), lambda bi, g, i: (bi, i, g)),
            pl.BlockSpec((None, tq, group * LANES), q_next),
            pl.BlockSpec((None, s, KA_W), lambda bi, g, i: (bi, 0, 0)),
            pl.BlockSpec((None, s, KA_W), lambda bi, g, i: (next_step(bi, g, i)[0], 0, 0)),
            pl.BlockSpec((None, HEAD_DIM, s), lambda bi, g, i: (bi, g, 0)),
        ],
        out_specs=pl.BlockSpec((None, tq, group * HEAD_DIM), lambda bi, g, i: (bi, i, g)),
        out_shape=jax.ShapeDtypeStruct((b, s, QA_W), BF16),
        scratch_shapes=[pltpu.VMEM((kv_tile, group * tq), F32)] * 2,
        compiler_params=_cparams("arbitrary", "arbitrary", "arbitrary"),
        name="gattn",
    )(qa, qa, ka, ka, vt)


NA_STEP_ROWS = 4
NA_WIN_ROWS = NA_STEP_ROWS + NB_ROWS


def _nattn_kernel(q_ref, k_ref, v_ref, tbl_ref, o_ref, *, rows):
    i = pl.program_id(1)
    ws = jnp.clip(NA_STEP_ROWS * i - NB_ROWS // 2, 0, rows - NA_WIN_ROWS)
    off = pl.multiple_of(ws * GRID_W, GRID_W)
    win = NA_WIN_ROWS * GRID_W
    tq = NA_STEP_ROWS * GRID_W
    low = lax.broadcasted_iota(jnp.int32, (tq, LANES), 1) < HEAD_DIM
    for pp in range(N_HEADS_B // 2):
        cols = slice(LANES * pp, LANES * (pp + 1))
        qp = q_ref[:, cols]
        zero = jnp.zeros_like(qp)
        q2 = jnp.concatenate([jnp.where(low, qp, zero), jnp.where(low, zero, qp)], axis=0)
        s = lax.dot_general(q2, k_ref[pl.ds(off, win), cols], NT_DIMS, preferred_element_type=F32)
        s = s + tbl_ref[pp]
        p = jnp.exp(s - jnp.max(s, axis=-1, keepdims=True))
        l = jnp.sum(p, axis=-1, keepdims=True)
        o2 = jnp.dot(p.astype(BF16), v_ref[pl.ds(off, win), cols], preferred_element_type=F32) / l
        o_ref[:, cols] = jnp.where(low, o2[:tq], o2[tq:]).astype(o_ref.dtype)


def _na_bias_table(rpb):
    qc = jnp.arange(GRID_W)
    kc = jnp.arange(GRID_W)
    cstart = jnp.clip(qc - NB_COLS // 2, 0, GRID_W - NB_COLS)
    valid = (kc[None, :] >= cstart[:, None]) & (kc[None, :] < cstart[:, None] + NB_COLS)
    pad = GRID_W - NB_COLS
    rp = jnp.pad(rpb.astype(F32), ((0, 0), (0, 0), (pad, pad)))
    toe = jnp.stack([rp[:, :, GRID_W - 1 - q:2 * GRID_W - 1 - q] for q in range(GRID_W)], axis=2)
    toe = jnp.where(valid[None, None], toe, NEG)

    def rows_for(rel, lo):
        d0 = lo - rel + NB_ROWS - 1
        own = toe[:, d0:d0 + NB_ROWS]
        neg = lambda n: jnp.full((N_HEADS_B, n, GRID_W, GRID_W), NEG, F32)
        full = jnp.concatenate([neg(lo), own, neg(NA_WIN_ROWS - NB_ROWS - lo)], axis=1)
        return full.transpose(0, 2, 1, 3).reshape(N_HEADS_B, GRID_W, NA_WIN_ROWS * GRID_W)

    kinds = []
    for rel0, lo_of in ((0, lambda a: 0), (NB_ROWS // 2, lambda a: a), (NB_ROWS, lambda a: NB_ROWS // 2)):
        t = jnp.stack([rows_for(rel0 + a, lo_of(a)) for a in range(NA_STEP_ROWS)], axis=1)
        kinds.append(t.reshape(N_HEADS_B // 2, 2 * NA_STEP_ROWS * GRID_W, NA_WIN_ROWS * GRID_W))
    return jnp.stack(kinds)


def _nattn(qb, kb, vb, rpb):
    b, s, w = qb.shape
    rows = s // GRID_W
    n_steps = rows // NA_STEP_ROWS
    assert rows % NA_STEP_ROWS == 0 and n_steps >= 3 and NA_STEP_ROWS == NB_ROWS // 2
    tbl = _na_bias_table(rpb)
    tq = NA_STEP_ROWS * GRID_W

    def kind_map(bi, i):
        return (jnp.where(i == 0, 0, jnp.where(i == n_steps - 1, 2, 1)), 0, 0, 0)

    return pl.pallas_call(
        functools.partial(_nattn_kernel, rows=rows),
        grid=(b, n_steps),
        in_specs=[
            pl.BlockSpec((None, tq, w), lambda bi, i: (bi, i, 0)),
            pl.BlockSpec((None, s, w), lambda bi, i: (bi, 0, 0)),
            pl.BlockSpec((None, s, w), lambda bi, i: (bi, 0, 0)),
            pl.BlockSpec((None, N_HEADS_B // 2, 2 * tq, NA_WIN_ROWS * GRID_W), kind_map),
        ],
        out_specs=pl.BlockSpec((None, tq, w), lambda bi, i: (bi, i, 0)),
        out_shape=jax.ShapeDtypeStruct((b, s, w), BF16),
        compiler_params=_cparams("arbitrary", "arbitrary"),
        name="nattn",
    )(qb, kb, vb, tbl)


def _memkv_kernel(mem_ref, g_ref, w_ref, k_ref, v_ref):
    m = _rms(mem_ref[...], g_ref[...]).astype(BF16)
    k_ref[...] = jnp.dot(m, w_ref[:, :D_MODEL], preferred_element_type=F32).astype(BF16)
    v_ref[...] = jnp.dot(m, w_ref[:, D_MODEL:], preferred_element_type=F32).astype(BF16)


def _memkv(mem, mem_norm, w_mkv_bf):
    b, m, d = mem.shape
    blk = pl.BlockSpec((None, m, d), lambda i: (i, 0, 0))
    return pl.pallas_call(
        _memkv_kernel,
        grid=(b,),
        in_specs=[blk, pl.BlockSpec((1, d), lambda i: (0, 0)), pl.BlockSpec((d, 2 * d), lambda i: (0, 0))],
        out_specs=(blk, blk),
        out_shape=(jax.ShapeDtypeStruct((b, m, d), BF16),) * 2,
        compiler_params=_cparams("arbitrary"),
        name="memkv",
    )(mem, mem_norm[None], w_mkv_bf)


def _mix_kernel(x_ref, ya_ref, yb_ref, mk_ref, mv_ref, ga_ref, gb_ref, wo_ref, gx_ref, wq_ref, wmo_ref,
                gf_ref, wr_hl_ref, wr_h_ref, x2_ref, h3_ref, aff_ref):
    na = _rms(ya_ref[...].astype(F32), ga_ref[...]).astype(BF16)
    nb = _rms(yb_ref[...].astype(F32), gb_ref[...]).astype(BF16)
    x1 = (x_ref[...]
          + jnp.dot(na, wo_ref[:QA_W, :], preferred_element_type=F32)
          + jnp.dot(nb, wo_ref[QA_W:, :], preferred_element_type=F32))

    h2 = _rms(x1, gx_ref[...]).astype(BF16)
    q = (jnp.dot(h2, wq_ref[...], preferred_element_type=F32) * (MEM_HEAD_DIM ** -0.5)).astype(BF16)
    heads = []
    for hh in range(MEM_HEADS):
        cols = slice(MEM_HEAD_DIM * hh, MEM_HEAD_DIM * (hh + 1))
        s = lax.dot_general(q[:, cols], mk_ref[:, cols], NT_DIMS, preferred_element_type=F32)
        m = jnp.max(s, axis=-1, keepdims=True)
        p = jnp.exp(s - m)
        l = jnp.sum(p, axis=-1, keepdims=True)
        heads.append((jnp.dot(p.astype(BF16), mv_ref[:, cols], preferred_element_type=F32) / l).astype(BF16))
    o = jnp.concatenate(heads, axis=1)
    x2 = x1 + jnp.dot(o, wmo_ref[...], preferred_element_type=F32)
    x2_ref[...] = x2

    h3 = _rms(x2, gf_ref[...])
    tm = h3.shape[0]
    for c in range(D_MODEL // LANES):
        h3_ref[pl.ds(c, tm, stride=D_MODEL // LANES), :] = h3[:, LANES * c:LANES * (c + 1)]
    hi, lo = _split_bf16(h3)
    r1 = lax.dot_general(wr_hl_ref[...], hi, NT_DIMS, preferred_element_type=F32)
    r2 = lax.dot_general(wr_h_ref[...], lo, NT_DIMS, preferred_element_type=F32)
    logits = r1[:N_EXPERTS] + r1[N_EXPERTS:] + r2
    e = jnp.exp(logits - jnp.max(logits, axis=0, keepdims=True))
    aff_ref[...] = e / jnp.sum(e, axis=0, keepdims=True)


def _mix(x, ya, yb, mk, mv, out_norm_a, out_norm_b, w_out_bf, xattn_norm, w_mq_bf, w_mo_bf, ffn_norm,
         w_router, tm):
    b, s, d = x.shape
    n = b * s
    tiles_per_seq = s // tm
    wr_t = w_router.T
    wr_hi, wr_lo = _split_bf16(wr_t)
    wr_hl = jnp.concatenate([wr_hi, wr_lo], axis=0)
    const = lambda shape: pl.BlockSpec(shape, lambda i: (0,) * len(shape))
    tok = lambda w: pl.BlockSpec((tm, w), lambda i: (i, 0))
    memb = pl.BlockSpec((None, N_MEM, d), lambda i: (i // tiles_per_seq, 0, 0))
    return pl.pallas_call(
        _mix_kernel,
        grid=(n // tm,),
        in_specs=[
            tok(d), tok(QA_W), tok(QB_W), memb, memb,
            const((1, QA_W)), const((1, QB_W)), const((d, d)),
            const((1, d)), const((d, d)), const((d, d)),
            const((1, d)), const((2 * N_EXPERTS, d)), const((N_EXPERTS, d)),
        ],
        out_specs=(tok(d), pl.BlockSpec((tm * (d // LANES), LANES), lambda i: (i, 0)),
                   pl.BlockSpec((N_EXPERTS, tm), lambda i: (0, i))),
        out_shape=(
            jax.ShapeDtypeStruct((n, d), F32),
            jax.ShapeDtypeStruct((n * (d // LANES), LANES), F32),
            jax.ShapeDtypeStruct((N_EXPERTS, n), F32),
        ),
        compiler_params=_cparams("arbitrary"),
        name="mix",
    )(x.reshape(n, d), ya.reshape(n, QA_W), yb.reshape(n, QB_W), mk, mv, out_norm_a[None], out_norm_b[None],
      w_out_bf, xattn_norm[None], w_mq_bf, w_mo_bf, ffn_norm[None], wr_hl, wr_hi)


def _front(x, mem, p, *, tm_in=512, tq=128, kv_tile=1024, tm_mix=512):
    qa, ka, vt, qb, kb, vb = _inproj(x, p["attn_norm"], p["w_in"], p["q_norm"], p["k_norm"], tm_in)
    ya = _gattn(qa, ka, vt, tq, min(kv_tile, x.shape[1] // 2))
    yb = _nattn(qb, kb, vb, p["rpb"])
    mk, mv = _memkv(mem, p["mem_norm"], p["w_mkv"])
    return _mix(x, ya, yb, mk, mv, p["out_norm_a"], p["out_norm_b"], p["w_out"], p["xattn_norm"],
                p["w_mq"], p["w_mo"], p["ffn_norm"], p["w_router"], tm_mix)


def _tri(n, strict_lower=False):
    i = lax.broadcasted_iota(jnp.int32, (n, n), 0)
    j = lax.broadcasted_iota(jnp.int32, (n, n), 1)
    return ((j < i) if strict_lower else (i <= j)).astype(BF16)


def _select_kernel(aff_ref, slot_ref, sel_ref, off_ref, *, cap):
    n_e, rows, _ = aff_ref.shape

    def bits_of(e):
        return lax.bitcast_convert_type(aff_ref[e], jnp.int32)

    def step(i, prefixes):
        bit = jnp.left_shift(jnp.int32(1), 30 - i)
        out = []
        for e in range(n_e):
            cand = prefixes[e] | bit
            cnt = jnp.sum(jnp.where(bits_of(e) >= cand, 1.0, 0.0), axis=(0, 1), keepdims=True)
            out.append(jnp.where(cnt >= float(cap), cand, prefixes[e]))
        return tuple(out)

    thr = lax.fori_loop(0, 31, step, tuple(jnp.zeros((1, 1), jnp.int32) for _ in range(n_e)))

    u_lane = _tri(LANES)
    u_rows = _tri(rows)
    l_rows = _tri(rows, strict_lower=True)
    ones_l = jnp.ones((LANES, LANES), BF16)
    ones_8 = jnp.ones((8, LANES), BF16)

    def prefix(mask_bf):
        within = jnp.dot(mask_bf, u_lane, preferred_element_type=F32)
        row_tot = jnp.dot(mask_bf, ones_l, preferred_element_type=F32)
        before = jnp.dot(l_rows, row_tot.astype(BF16), preferred_element_type=F32)
        return within, before

    for e in range(n_e):
        b = bits_of(e)
        gt = b > thr[e]
        eq = b == thr[e]
        need = float(cap) - jnp.sum(jnp.where(gt, 1.0, 0.0), axis=(0, 1), keepdims=True)
        eq_f = jnp.where(eq, 1.0, 0.0)
        within, before = prefix(eq_f.astype(BF16))
        sel = gt | (eq & ((before + within - eq_f) < need))
        sel_f = jnp.where(sel, 1.0, 0.0)
        sel_bf = sel_f.astype(BF16)
        within, before = prefix(sel_bf)
        rank = (before + within - 1.0).astype(jnp.int32)
        slot_ref[e] = jnp.where(sel, rank + e * cap, -1)
        sel_ref[e] = sel_bf
        cnt_row = lax.dot_general(ones_8, sel_bf, NT_DIMS, preferred_element_type=F32)
        end_row = jnp.dot(cnt_row.astype(BF16), u_rows, preferred_element_type=F32)
        off_ref[e] = end_row - cnt_row


def _select(aff3, cap):
    n_e, rows, _ = aff3.shape
    return pl.pallas_call(
        functools.partial(_select_kernel, cap=cap),
        out_shape=(
            jax.ShapeDtypeStruct((n_e, rows, LANES), jnp.int32),
            jax.ShapeDtypeStruct((n_e, rows, LANES), BF16),
            jax.ShapeDtypeStruct((n_e, 8, rows), F32),
        ),
        compiler_params=pltpu.CompilerParams(vmem_limit_bytes=VMEM_LIMIT),
        name="select",
    )(aff3)


def _compact_kernel(sel_ref, aff_ref, idx_ref, gate_ref, *, jt):
    rows = sel_ref.shape[0]
    sel = sel_ref[...]
    ones_8 = jnp.ones((8, LANES), BF16)
    cnt_row = lax.dot_general(ones_8, sel, NT_DIMS, preferred_element_type=F32)
    end_row = jnp.dot(cnt_row.astype(BF16), _tri(rows), preferred_element_type=F32)
    row_end = end_row[0:1, :]
    row_off = row_end - cnt_row[0:1, :]

    j0 = pl.program_id(1) * jt
    j_r = (j0 + lax.broadcasted_iota(jnp.int32, (jt, rows), 0)).astype(F32)
    below = jnp.where(row_end <= j_r, 1.0, 0.0).astype(BF16)
    hit = jnp.where((row_end > j_r) & (row_off <= j_r), 1.0, 0.0).astype(BF16)

    cnt_b = jnp.dot(sel, jnp.ones((LANES, LANES), BF16), preferred_element_type=F32)
    off_b = jnp.dot(below, cnt_b.astype(BF16), preferred_element_type=F32)
    sel_row = jnp.dot(hit, sel, preferred_element_type=F32)
    within = jnp.dot(sel_row.astype(BF16), _tri(LANES), preferred_element_type=F32)
    j_l = (j0 + lax.broadcasted_iota(jnp.int32, (jt, LANES), 0)).astype(F32)
    pick = (sel_row > 0.5) & (within == (j_l - off_b + 1.0))
    pick_bf = jnp.where(pick, 1.0, 0.0).astype(BF16)

    sub = lax.broadcasted_iota(jnp.int32, (8, LANES), 0)
    lane_id = jnp.where(sub == 0, lax.broadcasted_iota(jnp.int32, (8, LANES), 1), 0).astype(BF16)
    lane_pos = lax.dot_general(lane_id, pick_bf, NT_DIMS, preferred_element_type=F32)
    sub_r = lax.broadcasted_iota(jnp.int32, (8, rows), 0)
    r_id = lax.broadcasted_iota(jnp.int32, (8, rows), 1)
    r_split = jnp.where(sub_r == 0, r_id // 16, jnp.where(sub_r == 1, r_id % 16, 0)).astype(BF16)
    row_pos = lax.dot_general(r_split, hit, NT_DIMS, preferred_element_type=F32)
    tok = (row_pos[0:1, :] * 16.0 + row_pos[1:2, :]) * float(LANES) + lane_pos[0:1, :]
    idx_ref[...] = tok.astype(jnp.int32)

    a = aff_ref[...]
    a0 = a.astype(BF16)
    a1 = (a - a0.astype(F32)).astype(BF16)
    a2 = (a - a0.astype(F32) - a1.astype(F32)).astype(BF16)
    gate = jnp.zeros((1, jt), F32)
    for piece in (a0, a1, a2):
        g = jnp.dot(hit, piece, preferred_element_type=F32)
        g = jnp.where(pick, g, 0.0).astype(BF16)
        gate = gate + lax.dot_general(ones_8, g, NT_DIMS, preferred_element_type=F32)[0:1, :]
    gate_ref[...] = gate


def _compact(sel, aff3, cap, jt):
    n_e, rows, _ = sel.shape
    blk = pl.BlockSpec((None, rows, LANES), lambda e, j: (e, 0, 0))
    out_blk = pl.BlockSpec((None, 1, jt), lambda e, j: (e, 0, j))
    return pl.pallas_call(
        functools.partial(_compact_kernel, jt=jt),
        grid=(n_e, cap // jt),
        in_specs=[blk, blk],
        out_specs=(out_blk, out_blk),
        out_shape=(jax.ShapeDtypeStruct((n_e, 1, cap), jnp.int32), jax.ShapeDtypeStruct((n_e, 1, cap), F32)),
        compiler_params=_cparams("arbitrary", "arbitrary"),
        name="compact",
    )(sel, aff3)


def _ffn_kernel(idx_ref, nxt_ref, gate_ref, h_hbm, wg_ref, wu_ref, wd_ref, o_ref, xbuf, sems, *, fc):
    tile = D_MODEL // LANES
    ts = xbuf.shape[1] // tile
    g = pl.program_id(0)
    cur = g % 2

    def row_copy(half, j, token):
        return pltpu.make_async_copy(h_hbm.at[pl.ds(pl.multiple_of(token * tile, tile), tile), :],
                                     xbuf.at[half, pl.ds(j * tile, tile), :], sems.at[half])

    def gather(ids_ref, half):
        for j in range(ts):
            row_copy(half, j, ids_ref[0, j]).start()

    @pl.when(g == 0)
    def _():
        gather(idx_ref, 0)

    has_next = g + 1 < pl.num_programs(0)
    for half in range(2):
        @pl.when(has_next & (cur != half))
        def _(half=half):
            gather(nxt_ref, half)

    def drain(j, c):
        row_copy(cur, j, 0).wait()
        return c

    lax.fori_loop(0, ts, drain, 0, unroll=8)

    xe = jnp.concatenate([xbuf[cur, pl.ds(c, ts, stride=tile), :] for c in range(tile)], axis=1).astype(BF16)
    acc = jnp.zeros((ts, D_MODEL), F32)
    for c in range(D_FF // fc):
        cols = slice(fc * c, fc * (c + 1))
        g = jnp.dot(xe, wg_ref[:, cols], preferred_element_type=F32)
        u = jnp.dot(xe, wu_ref[:, cols], preferred_element_type=F32)
        he = (g / (1.0 + jnp.exp(-g)) * u).astype(BF16)
        acc = acc + jnp.dot(he, wd_ref[cols, :], preferred_element_type=F32)
    gate_col = jnp.broadcast_to(gate_ref[...], (LANES, ts)).T[:, 0:1]
    o_ref[...] = (acc * gate_col).astype(o_ref.dtype)


def _ffn(idx, gate, h3, wg, wu, wd, ts, fc):
    n_e, _, cap = idx.shape
    d = D_MODEL
    nt = cap // ts
    n_steps = n_e * nt
    idx3 = idx.reshape(n_steps, 1, ts)
    ids = lambda shift: pl.BlockSpec((None, 1, ts), lambda g: (jnp.minimum(g + shift, n_steps - 1), 0, 0),
                                     memory_space=pltpu.SMEM)
    wspec = lambda r, c: pl.BlockSpec((None, r, c), lambda g: (g // nt, 0, 0))
    return pl.pallas_call(
        functools.partial(_ffn_kernel, fc=fc),
        grid=(n_steps,),
        in_specs=[
            ids(0), ids(1),
            pl.BlockSpec((None, 1, ts), lambda g: (g // nt, 0, g % nt)),
            pl.BlockSpec(memory_space=pl.ANY),
            wspec(d, D_FF), wspec(d, D_FF), wspec(D_FF, d),
        ],
        out_specs=pl.BlockSpec((ts, d), lambda g: (g, 0)),
        out_shape=jax.ShapeDtypeStruct((n_e * cap, d), BF16),
        scratch_shapes=[pltpu.VMEM((2, ts * (d // LANES), LANES), F32), pltpu.SemaphoreType.DMA((2,))],
        compiler_params=_cparams("arbitrary"),
        name="ffn",
    )(idx3, idx3, gate, h3, wg, wu, wd)


COMBINE_WIN = 64


def _combine_kernel(base_ref, rounds_ref, x_ref, slot_ref, ye_hbm, g_ref, o_ref, buf, sems):
    n_e, t = slot_ref.shape
    total = ye_hbm.shape[0]
    tt = pl.program_id(0)
    cur = tt % 2
    w_iota = lax.broadcasted_iota(jnp.int32, (COMBINE_WIN, t), 0)

    def windows(tile, r):
        lows = [base_ref[e, tile] + r * COMBINE_WIN for e in range(n_e)]
        return lows, [pl.multiple_of(jnp.minimum(lo, total - COMBINE_WIN), 16) for lo in lows]

    def win_copy(half, e, start):
        return pltpu.make_async_copy(ye_hbm.at[pl.ds(start, COMBINE_WIN), :],
                                     buf.at[half, pl.ds(e * COMBINE_WIN, COMBINE_WIN), :], sems.at[half])

    def fetch(tile, r, half):
        _, starts = windows(tile, r)
        for e in range(n_e):
            win_copy(half, e, starts[e]).start()

    def scatter_add(r, acc):
        lows, starts = windows(tt, r)
        for e in range(n_e):
            win_copy(cur, e, starts[e]).wait()
        onehot_t = jnp.concatenate(
            [jnp.where((slot_ref[e:e + 1, :] == starts[e] + w_iota) & (slot_ref[e:e + 1, :] >= lows[e]),
                       1.0, 0.0).astype(BF16) for e in range(n_e)],
            axis=0)
        return acc + lax.dot_general(onehot_t, buf[cur], TN_DIMS, preferred_element_type=F32)

    @pl.when(tt == 0)
    def _():
        fetch(tt, 0, cur)

    @pl.when(tt + 1 < pl.num_programs(0))
    def _():
        fetch(tt + 1, 0, 1 - cur)

    y = scatter_add(0, jnp.zeros((t, D_MODEL), F32))

    def later_round(r, acc):
        fetch(tt, r, cur)
        return scatter_add(r, acc)

    y = lax.fori_loop(1, rounds_ref[tt], later_round, y)
    o_ref[...] = _rms(x_ref[...] + y, g_ref[...])


def _combine(x2, slot, ye, base, rounds, final_norm, t):
    n, d = x2.shape
    n_e = slot.shape[0]
    return pl.pallas_call(
        _combine_kernel,
        grid_spec=pltpu.PrefetchScalarGridSpec(
            num_scalar_prefetch=2,
            grid=(n // t,),
            in_specs=[
                pl.BlockSpec((t, d), lambda i, *_: (i, 0)),
                pl.BlockSpec((n_e, t), lambda i, *_: (0, i)),
                pl.BlockSpec(memory_space=pl.ANY),
                pl.BlockSpec((1, d), lambda i, *_: (0, 0)),
            ],
            out_specs=pl.BlockSpec((t, d), lambda i, *_: (i, 0)),
            scratch_shapes=[pltpu.VMEM((2, n_e * COMBINE_WIN, d), BF16), pltpu.SemaphoreType.DMA((2,))],
        ),
        out_shape=jax.ShapeDtypeStruct((n, d), F32),
        compiler_params=_cparams("arbitrary"),
        name="combine",
    )(base, rounds, x2, slot, ye, final_norm[None])


def _moe(x2, h3, aff, p, *, jt=512, ts=512, fc=512, t=256):
    n, d = x2.shape
    n_e = aff.shape[0]
    cap = EC_FACTOR * n // n_e
    rows = n // LANES
    aff3 = aff.reshape(n_e, rows, LANES)
    slot, sel, off = _select(aff3, cap)
    idx, gate = _compact(sel, aff3, cap, min(jt, cap))
    ye = _ffn(idx, gate, h3, p["w_gate"], p["w_up"], p["w_down"], min(ts, cap), fc)

    rows_per_tile = t // LANES
    first = off[:, 0, ::rows_per_tile].astype(jnp.int32) + cap * jnp.arange(n_e, dtype=jnp.int32)[:, None]
    nxt = jnp.concatenate([first[:, 1:], (cap * (jnp.arange(n_e, dtype=jnp.int32) + 1))[:, None]], axis=1)
    base = (first // 16) * 16
    rounds = jnp.max((nxt - base + COMBINE_WIN - 1) // COMBINE_WIN, axis=0).astype(jnp.int32)
    rounds = jnp.where(jnp.max(nxt - first, axis=0) > 0, rounds, 0)
    return _combine(x2, slot.reshape(n_e, n), ye, base, rounds, p["final_norm"], t)


def kernel(x_prompt, x_sample, mem_prompt, mem_sample, attn_norm, w_in, q_norm, k_norm, rpb, out_norm_a,
           out_norm_b, w_out, xattn_norm, mem_norm, w_mq, w_mkv, w_mo, ffn_norm, w_router, w_gate, w_up,
           w_down, final_norm):
    assert attn_norm.shape[0] == 1, "single-layer encoder"
    p = dict(attn_norm=attn_norm[0], w_in=w_in[0].astype(BF16), q_norm=q_norm[0], k_norm=k_norm[0],
             rpb=rpb[0], out_norm_a=out_norm_a[0], out_norm_b=out_norm_b[0], w_out=w_out[0].astype(BF16),
             xattn_norm=xattn_norm[0], mem_norm=mem_norm[0], w_mq=w_mq[0].astype(BF16),
             w_mkv=w_mkv[0].astype(BF16), w_mo=w_mo[0].astype(BF16), ffn_norm=ffn_norm[0],
             w_router=w_router[0], w_gate=w_gate[0].astype(BF16), w_up=w_up[0].astype(BF16),
             w_down=w_down[0].astype(BF16), final_norm=final_norm)
    outs = []
    for x, mem in ((x_prompt, mem_prompt), (x_sample, mem_sample)):
        x2, h3, aff = _front(x, mem, p)
        outs.append(_moe(x2, h3, aff, p).reshape(x.shape))
    return tuple(outs)
```

```python
import functools

import jax
import jax.numpy as jnp
from jax import lax
from jax.experimental import pallas as pl
from jax.experimental.pallas import tpu as pltpu

F32 = jnp.float32
BF16 = jnp.bfloat16

D_MODEL = 1024
GRID_W = 64
HEAD_DIM = 64
N_HEADS_A = 8
N_KV_A = 2
N_HEADS_B = 8
QA_W = N_HEADS_A * HEAD_DIM
KA_W = N_KV_A * HEAD_DIM
QB_W = N_HEADS_B * HEAD_DIM
D_IN = QA_W + 2 * KA_W + 3 * QB_W
ROPE_THETA = 10000.0
NB_ROWS = 8
NB_COLS = 16
N_MEM = 256
MEM_HEADS = 4
MEM_HEAD_DIM = D_MODEL // MEM_HEADS
N_EXPERTS = 16
EC_FACTOR = 2
D_FF = 2 * D_MODEL
EPS = 1e-6
NEG = -1e30
LOG2E = 1.4426950408889634

LANES = 128
VMEM_LIMIT = 56 * 1024 * 1024

NT_DIMS = (((1,), (1,)), ((), ()))
TN_DIMS = (((0,), (0,)), ((), ()))


def _cparams(*sem):
    return pltpu.CompilerParams(dimension_semantics=sem, vmem_limit_bytes=VMEM_LIMIT)


def _rms(x, g):
    return x * lax.rsqrt(jnp.mean(x * x, axis=-1, keepdims=True) + EPS) * g


def _split_bf16(x):
    hi = x.astype(BF16)
    lo = (x - hi.astype(F32)).astype(BF16)
    return hi, lo


def _head_norm_rope(z, gain, cos, sin_signed, blockdiag):
    w = z.shape[-1]
    hi, lo = _split_bf16(z * z)
    ss = (jnp.dot(hi, blockdiag, preferred_element_type=F32)
          + jnp.dot(lo, blockdiag, preferred_element_type=F32))
    y = z * lax.rsqrt(ss * (1.0 / HEAD_DIM) + EPS) * gain
    lane = lax.broadcasted_iota(jnp.int32, y.shape, 1)
    first = (lane % 32) < 16
    rot = jnp.where(first, pltpu.roll(y, w - 16, axis=1), pltpu.roll(y, 16, axis=1))
    return y * cos + rot * sin_signed


def _inproj_kernel(x_ref, g_ref, w_ref, wvt_ref, cos_ref, sin_ref, qg_ref, kg_ref, bdq_ref, bdk_ref,
                   qa_ref, ka_ref, vt_ref, qb_ref, kb_ref, vb_ref):
    h = _rms(x_ref[...], g_ref[...]).astype(BF16)

    def proj(lo, hi):
        return jnp.dot(h, w_ref[:, lo:hi], preferred_element_type=F32)

    cos = cos_ref[...]
    sin = sin_ref[...]
    q = _head_norm_rope(proj(0, QA_W), qg_ref[...], cos, sin, bdq_ref[...])
    k = _head_norm_rope(proj(QA_W, QA_W + KA_W), kg_ref[...], cos[:, :KA_W], sin[:, :KA_W],
                        bdk_ref[...])
    vt = lax.dot_general(wvt_ref[...], h, NT_DIMS, preferred_element_type=F32)

    lane = lax.broadcasted_iota(jnp.int32, (q.shape[0], LANES), 1)
    group = N_HEADS_A // N_KV_A
    for i in range(N_HEADS_A):
        src = q[:, LANES * (i // 2):LANES * (i // 2 + 1)]
        kv = i // group
        if (i % 2) != kv:
            src = pltpu.roll(src, HEAD_DIM, axis=1)
        keep = (lane >= HEAD_DIM * kv) & (lane < HEAD_DIM * (kv + 1))
        qa_ref[:, LANES * i:LANES * (i + 1)] = jnp.where(keep, src, 0.0).astype(BF16)
    ka_ref[...] = k.astype(BF16)
    vt_ref[...] = vt.astype(BF16)

    o = QA_W + 2 * KA_W
    qb_ref[...] = (proj(o, o + QB_W) * (HEAD_DIM ** -0.5 * LOG2E)).astype(BF16)
    kb_ref[...] = proj(o + QB_W, o + 2 * QB_W).astype(BF16)
    vb_ref[...] = proj(o + 2 * QB_W, o + 3 * QB_W).astype(BF16)


def _rope_tables(seq_len):
    half = HEAD_DIM // 2
    inv = ROPE_THETA ** (-jnp.arange(0, half, 2, dtype=F32) / half)
    t = jnp.arange(seq_len)
    row = (t // GRID_W).astype(F32)
    col = (t % GRID_W).astype(F32)
    ar = row[:, None] * inv[None]
    ac = col[:, None] * inv[None]
    ang = jnp.concatenate([ar, ar, ac, ac], axis=-1)
    sign = jnp.where((jnp.arange(HEAD_DIM) % 32) < 16, -1.0, 1.0).astype(F32)
    cos = jnp.tile(jnp.cos(ang), (1, N_HEADS_A))
    sin = jnp.tile(jnp.sin(ang) * sign[None], (1, N_HEADS_A))
    return cos, sin


def _blockdiag(width):
    i = jnp.arange(width) // HEAD_DIM
    return (i[:, None] == i[None, :]).astype(BF16)


def _inproj(x, attn_norm, w_in_bf, q_norm, k_norm, tm):
    b, s, d = x.shape
    cos, sin = _rope_tables(s)
    qg = (jnp.tile(q_norm, N_HEADS_A) * (HEAD_DIM ** -0.5 * LOG2E))[None]
    kg = jnp.tile(k_norm, N_KV_A)[None]
    const = lambda shape: pl.BlockSpec(shape, lambda i, j: (0,) * len(shape))
    tok = lambda w: pl.BlockSpec((None, tm, w), lambda i, j: (j, i, 0))
    out_shapes = (
        jax.ShapeDtypeStruct((b, s, N_HEADS_A * LANES), BF16),
        jax.ShapeDtypeStruct((b, s, KA_W), BF16),
        jax.ShapeDtypeStruct((b, KA_W, s), BF16),
        jax.ShapeDtypeStruct((b, s, QB_W), BF16),
        jax.ShapeDtypeStruct((b, s, QB_W), BF16),
        jax.ShapeDtypeStruct((b, s, QB_W), BF16),
    )
    return pl.pallas_call(
        _inproj_kernel,
        grid=(s // tm, b),
        in_specs=[
            tok(d),
            const((1, d)),
            const((d, D_IN)),
            const((KA_W, d)),
            pl.BlockSpec((tm, QA_W), lambda i, j: (i, 0)),
            pl.BlockSpec((tm, QA_W), lambda i, j: (i, 0)),
            const((1, QA_W)),
            const((1, KA_W)),
            const((QA_W, QA_W)),
            const((KA_W, KA_W)),
        ],
        out_specs=(
            tok(N_HEADS_A * LANES),
            tok(KA_W),
            pl.BlockSpec((None, KA_W, tm), lambda i, j: (j, 0, i)),
            tok(QB_W), tok(QB_W), tok(QB_W),
        ),
        out_shape=out_shapes,
        compiler_params=_cparams("arbitrary", "arbitrary"),
        name="inproj",
    )(x, attn_norm[None], w_in_bf, w_in_bf[:, QA_W + KA_W:QA_W + 2 * KA_W].T, cos, sin, qg, kg, _blockdiag(QA_W), _blockdiag(KA_W))


def _gattn_kernel(q_ref, qn_ref, k_ref, kn_ref, vt_ref, o_ref, s0_ref, s1_ref, *, kv_tile):
    tq = q_ref.shape[0]
    group = N_HEADS_A // N_KV_A
    stack_heads = lambda qb: jnp.concatenate([qb[:, LANES * i:LANES * (i + 1)] for i in range(group)], axis=0)
    q4 = stack_heads(q_ref[...])
    n_q = group * tq
    n_kv = k_ref.shape[0] // kv_tile
    ones = jnp.ones((16, kv_tile), BF16)

    def scores(j, dst, keys=k_ref, queries=q4):
        off = pl.multiple_of(j * kv_tile, kv_tile)
        dst[...] = lax.dot_general(keys[pl.ds(off, kv_tile), :], queries, NT_DIMS,
                                   preferred_element_type=F32)

    def consume(j, src, m, acc):
        off = pl.multiple_of(j * kv_tile, kv_tile)
        s = src[...]
        m_new = jnp.maximum(m, jnp.max(s, axis=0, keepdims=True))
        p = jnp.exp2(s - m_new).astype(BF16)
        va = jnp.concatenate([vt_ref[:, pl.ds(off, kv_tile)], ones], axis=0)
        return m_new, acc * jnp.exp2(m - m_new) + jnp.dot(va, p, preferred_element_type=F32)

    def body(i, carry):
        m, acc = carry
        scores(2 * i + 1, s1_ref)
        m, acc = consume(2 * i, s0_ref, m, acc)
        scores(2 * i + 2, s0_ref)
        return consume(2 * i + 1, s1_ref, m, acc)

    @pl.when((pl.program_id(0) == 0) & (pl.program_id(1) == 0) & (pl.program_id(2) == 0))
    def _():
        scores(0, s0_ref)

    init = (jnp.full((1, n_q), NEG, F32), jnp.zeros((HEAD_DIM + 16, n_q), F32))
    m, acc = lax.fori_loop(0, n_kv // 2 - 1, body, init)
    scores(n_kv - 1, s1_ref)
    m, acc = consume(n_kv - 2, s0_ref, m, acc)
    scores(0, s0_ref, kn_ref, stack_heads(qn_ref[...]))
    m, acc = consume(n_kv - 1, s1_ref, m, acc)
    ot = acc[:HEAD_DIM] / acc[HEAD_DIM:HEAD_DIM + 1]
    stacked = jnp.concatenate([ot[:, tq * i:tq * (i + 1)] for i in range(group)], axis=0)
    o_ref[...] = stacked.T.astype(o_ref.dtype)


def _gattn(qa, ka, vt, tq, kv_tile):
    b, s, _ = ka.shape
    group = N_HEADS_A // N_KV_A
    assert s % (2 * kv_tile) == 0 and s >= 2 * kv_tile
    n_i = s // tq
    n_steps = b * N_KV_A * n_i

    def next_step(bi, g, i):
        flat = jnp.minimum((bi * N_KV_A + g) * n_i + i + 1, n_steps - 1)
        return flat // (N_KV_A * n_i), (flat // n_i) % N_KV_A, flat % n_i

    def q_next(bi, g, i):
        nb, ng, ni = next_step(bi, g, i)
        return nb, ni, ng

    return pl.pallas_call(
        functools.partial(_gattn_kernel, kv_tile=kv_tile),
        grid=(b, N_KV_A, n_i),
        in_specs=[
            pl.BlockSpec((None, tq, group * LANES), lambda bi, g, i: (bi, i, g)),
            pl.BlockSpec((None, tq, group * LANES), q_next),
            pl.BlockSpec((None, s, KA_W), lambda bi, g, i: (bi, 0, 0)),
            pl.BlockSpec((None, s, KA_W), lambda bi, g, i: (next_step(bi, g, i)[0], 0, 0)),
            pl.BlockSpec((None, HEAD_DIM, s), lambda bi, g, i: (bi, g, 0)),
        ],
        out_specs=pl.BlockSpec((None, tq, group * HEAD_DIM), lambda bi, g, i: (bi, i, g)),
        out_shape=jax.ShapeDtypeStruct((b, s, QA_W), BF16),
        scratch_shapes=[pltpu.VMEM((kv_tile, group * tq), F32)] * 2,
        compiler_params=_cparams("arbitrary", "arbitrary", "arbitrary"),
        name="gattn",
    )(qa, qa, ka, ka, vt)


NA_STEP_ROWS = 4
NA_WIN_ROWS = NA_STEP_ROWS + NB_ROWS


def _nattn_kernel(q_ref, k_ref, v_ref, tbl_ref, o_ref, *, rows):
    i = pl.program_id(1)
    ws = jnp.clip(NA_STEP_ROWS * i - NB_ROWS // 2, 0, rows - NA_WIN_ROWS)
    off = pl.multiple_of(ws * GRID_W, GRID_W)
    win = NA_WIN_ROWS * GRID_W
    tq = NA_STEP_ROWS * GRID_W
    low = lax.broadcasted_iota(jnp.int32, (tq, LANES), 1) < HEAD_DIM
    for pp in range(N_HEADS_B // 2):
        cols = slice(LANES * pp, LANES * (pp + 1))
        qp = q_ref[:, cols]
        zero = jnp.zeros_like(qp)
        q2 = jnp.concatenate([jnp.where(low, qp, zero), jnp.where(low, zero, qp)], axis=0)
        s = lax.dot_general(q2, k_ref[pl.ds(off, win), cols], NT_DIMS, preferred_element_type=F32)
        s = s + tbl_ref[pp]
        p = jnp.exp2(s - jnp.max(s, axis=-1, keepdims=True))
        l = jnp.sum(p, axis=-1, keepdims=True)
        o2 = jnp.dot(p.astype(BF16), v_ref[pl.ds(off, win), cols], preferred_element_type=F32) / l
        o_ref[:, cols] = jnp.where(low, o2[:tq], o2[tq:]).astype(o_ref.dtype)


def _na_bias_table(rpb):
    qc = jnp.arange(GRID_W)
    kc = jnp.arange(GRID_W)
    cstart = jnp.clip(qc - NB_COLS // 2, 0, GRID_W - NB_COLS)
    valid = (kc[None, :] >= cstart[:, None]) & (kc[None, :] < cstart[:, None] + NB_COLS)
    pad = GRID_W - NB_COLS
    rp = jnp.pad(rpb.astype(F32) * LOG2E, ((0, 0), (0, 0), (pad, pad)))
    toe = jnp.stack([rp[:, :, GRID_W - 1 - q:2 * GRID_W - 1 - q] for q in range(GRID_W)], axis=2)
    toe = jnp.where(valid[None, None], toe, NEG)

    def rows_for(rel, lo):
        d0 = lo - rel + NB_ROWS - 1
        own = toe[:, d0:d0 + NB_ROWS]
        neg = lambda n: jnp.full((N_HEADS_B, n, GRID_W, GRID_W), NEG, F32)
        full = jnp.concatenate([neg(lo), own, neg(NA_WIN_ROWS - NB_ROWS - lo)], axis=1)
        return full.transpose(0, 2, 1, 3).reshape(N_HEADS_B, GRID_W, NA_WIN_ROWS * GRID_W)

    kinds = []
    for rel0, lo_of in ((0, lambda a: 0), (NB_ROWS // 2, lambda a: a), (NB_ROWS, lambda a: NB_ROWS // 2)):
        t = jnp.stack([rows_for(rel0 + a, lo_of(a)) for a in range(NA_STEP_ROWS)], axis=1)
        kinds.append(t.reshape(N_HEADS_B // 2, 2 * NA_STEP_ROWS * GRID_W, NA_WIN_ROWS * GRID_W))
    return jnp.stack(kinds)


def _nattn(qb, kb, vb, rpb):
    b, s, w = qb.shape
    rows = s // GRID_W
    n_steps = rows // NA_STEP_ROWS
    assert rows % NA_STEP_ROWS == 0 and n_steps >= 3 and NA_STEP_ROWS == NB_ROWS // 2
    tbl = _na_bias_table(rpb)
    tq = NA_STEP_ROWS * GRID_W

    def kind_map(bi, i):
        return (jnp.where(i == 0, 0, jnp.where(i == n_steps - 1, 2, 1)), 0, 0, 0)

    return pl.pallas_call(
        functools.partial(_nattn_kernel, rows=rows),
        grid=(b, n_steps),
        in_specs=[
            pl.BlockSpec((None, tq, w), lambda bi, i: (bi, i, 0)),
            pl.BlockSpec((None, s, w), lambda bi, i: (bi, 0, 0)),
            pl.BlockSpec((None, s, w), lambda bi, i: (bi, 0, 0)),
            pl.BlockSpec((None, N_HEADS_B // 2, 2 * tq, NA_WIN_ROWS * GRID_W), kind_map),
        ],
        out_specs=pl.BlockSpec((None, tq, w), lambda bi, i: (bi, i, 0)),
        out_shape=jax.ShapeDtypeStruct((b, s, w), BF16),
        compiler_params=_cparams("arbitrary", "arbitrary"),
        name="nattn",
    )(qb, kb, vb, tbl)


def _memkv_kernel(mem_ref, g_ref, w_ref, k_ref, v_ref):
    m = _rms(mem_ref[...], g_ref[...]).astype(BF16)
    k_ref[...] = jnp.dot(m, w_ref[:, :D_MODEL], preferred_element_type=F32).astype(BF16)
    v_ref[...] = jnp.dot(m, w_ref[:, D_MODEL:], preferred_element_type=F32).astype(BF16)


def _memkv(mem, mem_norm, w_mkv_bf):
    b, m, d = mem.shape
    blk = pl.BlockSpec((None, m, d), lambda i: (i, 0, 0))
    return pl.pallas_call(
        _memkv_kernel,
        grid=(b,),
        in_specs=[blk, pl.BlockSpec((1, d), lambda i: (0, 0)), pl.BlockSpec((d, 2 * d), lambda i: (0, 0))],
        out_specs=(blk, blk),
        out_shape=(jax.ShapeDtypeStruct((b, m, d), BF16),) * 2,
        compiler_params=_cparams("arbitrary"),
        name="memkv",
    )(mem, mem_norm[None], w_mkv_bf)


def _mix_kernel(x_ref, ya_ref, yb_ref, mk_ref, mv_ref, ga_ref, gb_ref, wo_ref, gx_ref, wq_ref, wmo_ref,
                gf_ref, wr_hl_ref, wr_h_ref, x2_ref, h3_ref, aff_ref):
    na = _rms(ya_ref[...].astype(F32), ga_ref[...]).astype(BF16)
    nb = _rms(yb_ref[...].astype(F32), gb_ref[...]).astype(BF16)
    x1 = (x_ref[...]
          + jnp.dot(na, wo_ref[:QA_W, :], preferred_element_type=F32)
          + jnp.dot(nb, wo_ref[QA_W:, :], preferred_element_type=F32))

    h2 = _rms(x1, gx_ref[...]).astype(BF16)
    q = (jnp.dot(h2, wq_ref[...], preferred_element_type=F32) * (MEM_HEAD_DIM ** -0.5 * LOG2E)).astype(BF16)
    heads = []
    for hh in range(MEM_HEADS):
        cols = slice(MEM_HEAD_DIM * hh, MEM_HEAD_DIM * (hh + 1))
        s = lax.dot_general(q[:, cols], mk_ref[:, cols], NT_DIMS, preferred_element_type=F32)
        m = jnp.max(s, axis=-1, keepdims=True)
        p = jnp.exp2(s - m)
        l = jnp.sum(p, axis=-1, keepdims=True)
        heads.append((jnp.dot(p.astype(BF16), mv_ref[:, cols], preferred_element_type=F32) / l).astype(BF16))
    o = jnp.concatenate(heads, axis=1)
    x2 = x1 + jnp.dot(o, wmo_ref[...], preferred_element_type=F32)
    x2_ref[...] = x2

    h3 = _rms(x2, gf_ref[...])
    tm = h3.shape[0]
    for c in range(D_MODEL // LANES):
        h3_ref[pl.ds(c, tm, stride=D_MODEL // LANES), :] = h3[:, LANES * c:LANES * (c + 1)]
    hi, lo = _split_bf16(h3)
    r1 = lax.dot_general(wr_hl_ref[...], hi, NT_DIMS, preferred_element_type=F32)
    r2 = lax.dot_general(wr_h_ref[...], lo, NT_DIMS, preferred_element_type=F32)
    logits = r1[:N_EXPERTS] + r1[N_EXPERTS:] + r2
    e = jnp.exp(logits - jnp.max(logits, axis=0, keepdims=True))
    aff_ref[...] = e / jnp.sum(e, axis=0, keepdims=True)


def _mix(x, ya, yb, mk, mv, out_norm_a, out_norm_b, w_out_bf, xattn_norm, w_mq_bf, w_mo_bf, ffn_norm,
         w_router, tm):
    b, s, d = x.shape
    n = b * s
    tiles_per_seq = s // tm
    wr_t = w_router.T
    wr_hi, wr_lo = _split_bf16(wr_t)
    wr_hl = jnp.concatenate([wr_hi, wr_lo], axis=0)
    const = lambda shape: pl.BlockSpec(shape, lambda i: (0,) * len(shape))
    tok = lambda w: pl.BlockSpec((tm, w), lambda i: (i, 0))
    memb = pl.BlockSpec((None, N_MEM, d), lambda i: (i // tiles_per_seq, 0, 0))
    return pl.pallas_call(
        _mix_kernel,
        grid=(n // tm,),
        in_specs=[
            tok(d), tok(QA_W), tok(QB_W), memb, memb,
            const((1, QA_W)), const((1, QB_W)), const((d, d)),
            const((1, d)), const((d, d)), const((d, d)),
            const((1, d)), const((2 * N_EXPERTS, d)), const((N_EXPERTS, d)),
        ],
        out_specs=(tok(d), pl.BlockSpec((tm * (d // LANES), LANES), lambda i: (i, 0)),
                   pl.BlockSpec((N_EXPERTS, tm), lambda i: (0, i))),
        out_shape=(
            jax.ShapeDtypeStruct((n, d), F32),
            jax.ShapeDtypeStruct((n * (d // LANES), LANES), F32),
            jax.ShapeDtypeStruct((N_EXPERTS, n), F32),
        ),
        compiler_params=_cparams("arbitrary"),
        name="mix",
    )(x.reshape(n, d), ya.reshape(n, QA_W), yb.reshape(n, QB_W), mk, mv, out_norm_a[None], out_norm_b[None],
      w_out_bf, xattn_norm[None], w_mq_bf, w_mo_bf, ffn_norm[None], wr_hl, wr_hi)


def _front(x, mem, p, *, tm_in=512, tq=128, kv_tile=1024, tm_mix=512):
    qa, ka, vt, qb, kb, vb = _inproj(x, p["attn_norm"], p["w_in"], p["q_norm"], p["k_norm"], tm_in)
    ya = _gattn(qa, ka, vt, tq, min(kv_tile, x.shape[1] // 2))
    yb = _nattn(qb, kb, vb, p["rpb"])
    mk, mv = _memkv(mem, p["mem_norm"], p["w_mkv"])
    return _mix(x, ya, yb, mk, mv, p["out_norm_a"], p["out_norm_b"], p["w_out"], p["xattn_norm"],
                p["w_mq"], p["w_mo"], p["ffn_norm"], p["w_router"], tm_mix)


def _tri(n, strict_lower=False):
    i = lax.broadcasted_iota(jnp.int32, (n, n), 0)
    j = lax.broadcasted_iota(jnp.int32, (n, n), 1)
    return ((j < i) if strict_lower else (i <= j)).astype(BF16)


def _select_kernel(aff_ref, slot_ref, sel_ref, off_ref, *, cap):
    n_e, rows, _ = aff_ref.shape

    def bits_of(e):
        return lax.bitcast_convert_type(aff_ref[e], jnp.int32)

    def step(i, prefixes):
        bit = jnp.left_shift(jnp.int32(1), 30 - i)
        out = []
        for e in range(n_e):
            cand = prefixes[e] | bit
            cnt = jnp.sum(jnp.where(bits_of(e) >= cand, 1.0, 0.0), axis=(0, 1), keepdims=True)
            out.append(jnp.where(cnt >= float(cap), cand, prefixes[e]))
        return tuple(out)

    thr = lax.fori_loop(0, 31, step, tuple(jnp.zeros((1, 1), jnp.int32) for _ in range(n_e)))

    u_lane = _tri(LANES)
    u_rows = _tri(rows)
    l_rows = _tri(rows, strict_lower=True)
    ones_l = jnp.ones((LANES, LANES), BF16)
    ones_8 = jnp.ones((8, LANES), BF16)

    def prefix(mask_bf):
        within = jnp.dot(mask_bf, u_lane, preferred_element_type=F32)
        row_tot = jnp.dot(mask_bf, ones_l, preferred_element_type=F32)
        before = jnp.dot(l_rows, row_tot.astype(BF16), preferred_element_type=F32)
        return within, before

    for e in range(n_e):
        b = bits_of(e)
        gt = b > thr[e]
        eq = b == thr[e]
        need = float(cap) - jnp.sum(jnp.where(gt, 1.0, 0.0), axis=(0, 1), keepdims=True)
        eq_f = jnp.where(eq, 1.0, 0.0)
        within, before = prefix(eq_f.astype(BF16))
        sel = gt | (eq & ((before + within - eq_f) < need))
        sel_f = jnp.where(sel, 1.0, 0.0)
        sel_bf = sel_f.astype(BF16)
        within, before = prefix(sel_bf)
        rank = (before + within - 1.0).astype(jnp.int32)
        slot_ref[e] = jnp.where(sel, rank + e * cap, -1)
        sel_ref[e] = sel_bf
        cnt_row = lax.dot_general(ones_8, sel_bf, NT_DIMS, preferred_element_type=F32)
        end_row = jnp.dot(cnt_row.astype(BF16), u_rows, preferred_element_type=F32)
        off_ref[e] = end_row - cnt_row


def _select(aff3, cap):
    n_e, rows, _ = aff3.shape
    return pl.pallas_call(
        functools.partial(_select_kernel, cap=cap),
        out_shape=(
            jax.ShapeDtypeStruct((n_e, rows, LANES), jnp.int32),
            jax.ShapeDtypeStruct((n_e, rows, LANES), BF16),
            jax.ShapeDtypeStruct((n_e, 8, rows), F32),
        ),
        compiler_params=pltpu.CompilerParams(vmem_limit_bytes=VMEM_LIMIT),
        name="select",
    )(aff3)


def _compact_kernel(sel_ref, aff_ref, idx_ref, gate_ref, *, jt):
    rows = sel_ref.shape[0]
    sel = sel_ref[...]
    ones_8 = jnp.ones((8, LANES), BF16)
    cnt_row = lax.dot_general(ones_8, sel, NT_DIMS, preferred_element_type=F32)
    end_row = jnp.dot(cnt_row.astype(BF16), _tri(rows), preferred_element_type=F32)
    row_end = end_row[0:1, :]
    row_off = row_end - cnt_row[0:1, :]

    j0 = pl.program_id(1) * jt
    j_r = (j0 + lax.broadcasted_iota(jnp.int32, (jt, rows), 0)).astype(F32)
    below = jnp.where(row_end <= j_r, 1.0, 0.0).astype(BF16)
    hit = jnp.where((row_end > j_r) & (row_off <= j_r), 1.0, 0.0).astype(BF16)

    a = aff_ref[...]
    a0 = a.astype(BF16)
    a1 = (a - a0.astype(F32)).astype(BF16)
    a2 = (a - a0.astype(F32) - a1.astype(F32)).astype(BF16)
    picked = jnp.dot(hit, jnp.concatenate([sel, a0, a1, a2], axis=1), preferred_element_type=F32)

    cnt_b = jnp.dot(sel, jnp.ones((LANES, LANES), BF16), preferred_element_type=F32)
    off_b = jnp.dot(below, cnt_b.astype(BF16), preferred_element_type=F32)
    sel_row = picked[:, :LANES]
    within = jnp.dot(sel_row.astype(BF16), _tri(LANES), preferred_element_type=F32)
    j_l = (j0 + lax.broadcasted_iota(jnp.int32, (jt, LANES), 0)).astype(F32)
    pick = (sel_row > 0.5) & (within == (j_l - off_b + 1.0))
    pick_bf = jnp.where(pick, 1.0, 0.0).astype(BF16)

    sub = lax.broadcasted_iota(jnp.int32, (8, LANES), 0)
    lane_id = jnp.where(sub == 0, lax.broadcasted_iota(jnp.int32, (8, LANES), 1), 0).astype(BF16)
    lane_pos = lax.dot_general(lane_id, pick_bf, NT_DIMS, preferred_element_type=F32)
    sub_r = lax.broadcasted_iota(jnp.int32, (8, rows), 0)
    r_id = lax.broadcasted_iota(jnp.int32, (8, rows), 1)
    r_split = jnp.where(sub_r == 0, r_id // 16, jnp.where(sub_r == 1, r_id % 16, 0)).astype(BF16)
    row_pos = lax.dot_general(r_split, hit, NT_DIMS, preferred_element_type=F32)
    tok = (row_pos[0:1, :] * 16.0 + row_pos[1:2, :]) * float(LANES) + lane_pos[0:1, :]
    idx_ref[...] = tok.astype(jnp.int32)

    gate = jnp.zeros((1, jt), F32)
    for i in range(1, 4):
        g = jnp.where(pick, picked[:, LANES * i:LANES * (i + 1)], 0.0).astype(BF16)
        gate = gate + lax.dot_general(ones_8, g, NT_DIMS, preferred_element_type=F32)[0:1, :]
    gate_ref[...] = gate


def _compact(sel, aff3, cap, jt):
    n_e, rows, _ = sel.shape
    blk = pl.BlockSpec((None, rows, LANES), lambda e, j: (e, 0, 0))
    out_blk = pl.BlockSpec((None, 1, jt), lambda e, j: (e, 0, j))
    return pl.pallas_call(
        functools.partial(_compact_kernel, jt=jt),
        grid=(n_e, cap // jt),
        in_specs=[blk, blk],
        out_specs=(out_blk, out_blk),
        out_shape=(jax.ShapeDtypeStruct((n_e, 1, cap), jnp.int32), jax.ShapeDtypeStruct((n_e, 1, cap), F32)),
        compiler_params=_cparams("arbitrary", "arbitrary"),
        name="compact",
    )(sel, aff3)


def _ffn_kernel(idx_ref, nxt_ref, gate_ref, h_hbm, wg_ref, wu_ref, wd_ref, o_ref, xbuf, sems, *, fc):
    tile = D_MODEL // LANES
    ts = xbuf.shape[1] // tile
    g = pl.program_id(0)
    cur = g % 2

    def row_copy(half, j, token):
        return pltpu.make_async_copy(h_hbm.at[pl.ds(pl.multiple_of(token * tile, tile), tile), :],
                                     xbuf.at[half, pl.ds(j * tile, tile), :], sems.at[half])

    def gather(ids_ref, half):
        for j in range(ts):
            row_copy(half, j, ids_ref[0, j]).start()

    @pl.when(g == 0)
    def _():
        gather(idx_ref, 0)

    has_next = g + 1 < pl.num_programs(0)
    for half in range(2):
        @pl.when(has_next & (cur != half))
        def _(half=half):
            gather(nxt_ref, half)

    def drain(j, c):
        row_copy(cur, j, 0).wait()
        return c

    lax.fori_loop(0, ts, drain, 0, unroll=8)

    xe = jnp.concatenate([xbuf[cur, pl.ds(c, ts, stride=tile), :] for c in range(tile)], axis=1).astype(BF16)
    acc = jnp.zeros((ts, D_MODEL), F32)
    for c in range(D_FF // fc):
        cols = slice(fc * c, fc * (c + 1))
        g = jnp.dot(xe, wg_ref[:, cols], preferred_element_type=F32)
        u = jnp.dot(xe, wu_ref[:, cols], preferred_element_type=F32)
        he = (g / (1.0 + jnp.exp(-g)) * u).astype(BF16)
        acc = acc + jnp.dot(he, wd_ref[cols, :], preferred_element_type=F32)
    gate_col = jnp.broadcast_to(gate_ref[...], (LANES, ts)).T[:, 0:1]
    o_ref[...] = (acc * gate_col).astype(o_ref.dtype)


def _ffn(idx, gate, h3, wg, wu, wd, ts, fc):
    n_e, _, cap = idx.shape
    d = D_MODEL
    nt = cap // ts
    n_steps = n_e * nt
    idx3 = idx.reshape(n_steps, 1, ts)
    ids = lambda shift: pl.BlockSpec((None, 1, ts), lambda g: (jnp.minimum(g + shift, n_steps - 1), 0, 0),
                                     memory_space=pltpu.SMEM)
    wspec = lambda r, c: pl.BlockSpec((None, r, c), lambda g: (g // nt, 0, 0))
    return pl.pallas_call(
        functools.partial(_ffn_kernel, fc=fc),
        grid=(n_steps,),
        in_specs=[
            ids(0), ids(1),
            pl.BlockSpec((None, 1, ts), lambda g: (g // nt, 0, g % nt)),
            pl.BlockSpec(memory_space=pl.ANY),
            wspec(d, D_FF), wspec(d, D_FF), wspec(D_FF, d),
        ],
        out_specs=pl.BlockSpec((ts, d), lambda g: (g, 0)),
        out_shape=jax.ShapeDtypeStruct((n_e * cap, d), BF16),
        scratch_shapes=[pltpu.VMEM((2, ts * (d // LANES), LANES), F32), pltpu.SemaphoreType.DMA((2,))],
        compiler_params=_cparams("arbitrary"),
        name="ffn",
    )(idx3, idx3, gate, h3, wg, wu, wd)


COMBINE_WIN = 64


def _combine_kernel(base_ref, rounds_ref, x_ref, slot_ref, ye_hbm, g_ref, o_ref, buf, sems):
    n_e, t = slot_ref.shape
    total = ye_hbm.shape[0]
    tt = pl.program_id(0)
    cur = tt % 2
    w_iota = lax.broadcasted_iota(jnp.int32, (COMBINE_WIN, t), 0)

    def windows(tile, r):
        lows = [base_ref[e, tile] + r * COMBINE_WIN for e in range(n_e)]
        return lows, [pl.multiple_of(jnp.minimum(lo, total - COMBINE_WIN), 16) for lo in lows]

    def win_copy(half, e, start):
        return pltpu.make_async_copy(ye_hbm.at[pl.ds(start, COMBINE_WIN), :],
                                     buf.at[half, pl.ds(e * COMBINE_WIN, COMBINE_WIN), :], sems.at[half])

    def fetch(tile, r, half):
        _, starts = windows(tile, r)
        for e in range(n_e):
            win_copy(half, e, starts[e]).start()

    def scatter_add(r, acc):
        lows, starts = windows(tt, r)
        for e in range(n_e):
            win_copy(cur, e, starts[e]).wait()
        onehot_t = jnp.concatenate(
            [jnp.where((slot_ref[e:e + 1, :] == starts[e] + w_iota) & (slot_ref[e:e + 1, :] >= lows[e]),
                       1.0, 0.0).astype(BF16) for e in range(n_e)],
            axis=0)
        return acc + lax.dot_general(onehot_t, buf[cur], TN_DIMS, preferred_element_type=F32)

    @pl.when(tt == 0)
    def _():
        fetch(tt, 0, cur)

    @pl.when(tt + 1 < pl.num_programs(0))
    def _():
        fetch(tt + 1, 0, 1 - cur)

    y = scatter_add(0, jnp.zeros((t, D_MODEL), F32))

    def later_round(r, acc):
        fetch(tt, r, cur)
        return scatter_add(r, acc)

    y = lax.fori_loop(1, rounds_ref[tt], later_round, y)
    o_ref[...] = _rms(x_ref[...] + y, g_ref[...])


def _combine(x2, slot, ye, base, rounds, final_norm, t):
    n, d = x2.shape
    n_e = slot.shape[0]
    return pl.pallas_call(
        _combine_kernel,
        grid_spec=pltpu.PrefetchScalarGridSpec(
            num_scalar_prefetch=2,
            grid=(n // t,),
            in_specs=[
                pl.BlockSpec((t, d), lambda i, *_: (i, 0)),
                pl.BlockSpec((n_e, t), lambda i, *_: (0, i)),
                pl.BlockSpec(memory_space=pl.ANY),
                pl.BlockSpec((1, d), lambda i, *_: (0, 0)),
            ],
            out_specs=pl.BlockSpec((t, d), lambda i, *_: (i, 0)),
            scratch_shapes=[pltpu.VMEM((2, n_e * COMBINE_WIN, d), BF16), pltpu.SemaphoreType.DMA((2,))],
        ),
        out_shape=jax.ShapeDtypeStruct((n, d), F32),
        compiler_params=_cparams("arbitrary"),
        name="combine",
    )(base, rounds, x2, slot, ye, final_norm[None])


def _moe(x2, h3, aff, p, *, jt=512, ts=512, fc=512, t=256):
    n, d = x2.shape
    n_e = aff.shape[0]
    cap = EC_FACTOR * n // n_e
    rows = n // LANES
    aff3 = aff.reshape(n_e, rows, LANES)
    slot, sel, off = _select(aff3, cap)
    idx, gate = _compact(sel, aff3, cap, min(jt, cap))
    ye = _ffn(idx, gate, h3, p["w_gate"], p["w_up"], p["w_down"], min(ts, cap), fc)

    rows_per_tile = t // LANES
    first = off[:, 0, ::rows_per_tile].astype(jnp.int32) + cap * jnp.arange(n_e, dtype=jnp.int32)[:, None]
    nxt = jnp.concatenate([first[:, 1:], (cap * (jnp.arange(n_e, dtype=jnp.int32) + 1))[:, None]], axis=1)
    base = (first // 16) * 16
    rounds = jnp.max((nxt - base + COMBINE_WIN - 1) // COMBINE_WIN, axis=0).astype(jnp.int32)
    rounds = jnp.where(jnp.max(nxt - first, axis=0) > 0, rounds, 0)
    return _combine(x2, slot.reshape(n_e, n), ye, base, rounds, p["final_norm"], t)


def kernel(x_prompt, x_sample, mem_prompt, mem_sample, attn_norm, w_in, q_norm, k_norm, rpb, out_norm_a,
           out_norm_b, w_out, xattn_norm, mem_norm, w_mq, w_mkv, w_mo, ffn_norm, w_router, w_gate, w_up,
           w_down, final_norm):
    assert attn_norm.shape[0] == 1, "single-layer encoder"
    p = dict(attn_norm=attn_norm[0], w_in=w_in[0].astype(BF16), q_norm=q_norm[0], k_norm=k_norm[0],
             rpb=rpb[0], out_norm_a=out_norm_a[0], out_norm_b=out_norm_b[0], w_out=w_out[0].astype(BF16),
             xattn_norm=xattn_norm[0], mem_norm=mem_norm[0], w_mq=w_mq[0].astype(BF16),
             w_mkv=w_mkv[0].astype(BF16), w_mo=w_mo[0].astype(BF16), ffn_norm=ffn_norm[0],
             w_router=w_router[0], w_gate=w_gate[0].astype(BF16), w_up=w_up[0].astype(BF16),
             w_down=w_down[0].astype(BF16), final_norm=final_norm)
    outs = []
    for x, mem in ((x_prompt, mem_prompt), (x_sample, mem_sample)):
        x2, h3, aff = _front(x, mem, p)
        outs.append(_moe(x2, h3, aff, p).reshape(x.shape))
    return tuple(outs)
```

```python
import functools

import jax
import jax.numpy as jnp
from jax import lax
from jax.experimental import pallas as pl
from jax.experimental.pallas import tpu as pltpu

F32 = jnp.float32
BF16 = jnp.bfloat16

D_MODEL = 1024
GRID_W = 64
HEAD_DIM = 64
N_HEADS_A = 8
N_KV_A = 2
N_HEADS_B = 8
QA_W = N_HEADS_A * HEAD_DIM
KA_W = N_KV_A * HEAD_DIM
QB_W = N_HEADS_B * HEAD_DIM
D_IN = QA_W + 2 * KA_W + 3 * QB_W
ROPE_THETA = 10000.0
NB_ROWS = 8
NB_COLS = 16
N_MEM = 256
MEM_HEADS = 4
MEM_HEAD_DIM = D_MODEL // MEM_HEADS
N_EXPERTS = 16
EC_FACTOR = 2
D_FF = 2 * D_MODEL
EPS = 1e-6
NEG = -1e30
LOG2E = 1.4426950408889634

LANES = 128
VMEM_LIMIT = 56 * 1024 * 1024

NT_DIMS = (((1,), (1,)), ((), ()))
TN_DIMS = (((0,), (0,)), ((), ()))


def _cparams(*sem):
    return pltpu.CompilerParams(dimension_semantics=sem, vmem_limit_bytes=VMEM_LIMIT)


def _rms(x, g):
    return x * lax.rsqrt(jnp.mean(x * x, axis=-1, keepdims=True) + EPS) * g


def _split_bf16(x):
    hi = x.astype(BF16)
    lo = (x - hi.astype(F32)).astype(BF16)
    return hi, lo


def _head_norm_rope(z, gain, cos, sin_signed, blockdiag):
    w = z.shape[-1]
    ss = jnp.dot((z * z).astype(BF16), blockdiag, preferred_element_type=F32)
    y = z * lax.rsqrt(ss * (1.0 / HEAD_DIM) + EPS) * gain
    lane = lax.broadcasted_iota(jnp.int32, y.shape, 1)
    first = (lane % 32) < 16
    rot = jnp.where(first, pltpu.roll(y, w - 16, axis=1), pltpu.roll(y, 16, axis=1))
    return y * cos + rot * sin_signed


def _inproj_kernel(x_ref, g_ref, w_ref, wvt_ref, cos_ref, sin_ref, qg_ref, kg_ref, bdq_ref, bdk_ref,
                   qa_ref, ka_ref, vt_ref, qb_ref, kb_ref, vb_ref):
    h = _rms(x_ref[...], g_ref[...]).astype(BF16)

    def proj(lo, hi):
        return jnp.dot(h, w_ref[:, lo:hi], preferred_element_type=F32)

    cos = cos_ref[...]
    sin = sin_ref[...]
    q = _head_norm_rope(proj(0, QA_W), qg_ref[...], cos, sin, bdq_ref[...])
    k = _head_norm_rope(proj(QA_W, QA_W + KA_W), kg_ref[...], cos[:, :KA_W], sin[:, :KA_W],
                        bdk_ref[...])
    vt = lax.dot_general(wvt_ref[...], h, NT_DIMS, preferred_element_type=F32)

    lane = lax.broadcasted_iota(jnp.int32, (q.shape[0], LANES), 1)
    group = N_HEADS_A // N_KV_A
    for i in range(N_HEADS_A):
        src = q[:, LANES * (i // 2):LANES * (i // 2 + 1)]
        kv = i // group
        if (i % 2) != kv:
            src = pltpu.roll(src, HEAD_DIM, axis=1)
        keep = (lane >= HEAD_DIM * kv) & (lane < HEAD_DIM * (kv + 1))
        qa_ref[:, LANES * i:LANES * (i + 1)] = jnp.where(keep, src, 0.0).astype(BF16)
    ka_ref[...] = k.astype(BF16)
    vt_ref[...] = vt.astype(BF16)

    o = QA_W + 2 * KA_W
    qb_ref[...] = (proj(o, o + QB_W) * (HEAD_DIM ** -0.5 * LOG2E)).astype(BF16)
    kb_ref[...] = proj(o + QB_W, o + 2 * QB_W).astype(BF16)
    vb_ref[...] = proj(o + 2 * QB_W, o + 3 * QB_W).astype(BF16)


def _rope_tables(seq_len):
    half = HEAD_DIM // 2
    inv = ROPE_THETA ** (-jnp.arange(0, half, 2, dtype=F32) / half)
    t = jnp.arange(seq_len)
    row = (t // GRID_W).astype(F32)
    col = (t % GRID_W).astype(F32)
    ar = row[:, None] * inv[None]
    ac = col[:, None] * inv[None]
    ang = jnp.concatenate([ar, ar, ac, ac], axis=-1)
    sign = jnp.where((jnp.arange(HEAD_DIM) % 32) < 16, -1.0, 1.0).astype(F32)
    cos = jnp.tile(jnp.cos(ang), (1, N_HEADS_A))
    sin = jnp.tile(jnp.sin(ang) * sign[None], (1, N_HEADS_A))
    return cos, sin


def _blockdiag(width):
    i = jnp.arange(width) // HEAD_DIM
    return (i[:, None] == i[None, :]).astype(BF16)


def _inproj(x, attn_norm, w_in_bf, q_norm, k_norm, tm):
    b, s, d = x.shape
    cos, sin = _rope_tables(s)
    qg = (jnp.tile(q_norm, N_HEADS_A) * (HEAD_DIM ** -0.5 * LOG2E))[None]
    kg = jnp.tile(k_norm, N_KV_A)[None]
    const = lambda shape: pl.BlockSpec(shape, lambda i, j: (0,) * len(shape))
    tok = lambda w: pl.BlockSpec((None, tm, w), lambda i, j: (j, i, 0))
    out_shapes = (
        jax.ShapeDtypeStruct((b, s, N_HEADS_A * LANES), BF16),
        jax.ShapeDtypeStruct((b, s, KA_W), BF16),
        jax.ShapeDtypeStruct((b, KA_W, s), BF16),
        jax.ShapeDtypeStruct((b, s, QB_W), BF16),
        jax.ShapeDtypeStruct((b, s, QB_W), BF16),
        jax.ShapeDtypeStruct((b, s, QB_W), BF16),
    )
    return pl.pallas_call(
        _inproj_kernel,
        grid=(s // tm, b),
        in_specs=[
            tok(d),
            const((1, d)),
            const((d, D_IN)),
            const((KA_W, d)),
            pl.BlockSpec((tm, QA_W), lambda i, j: (i, 0)),
            pl.BlockSpec((tm, QA_W), lambda i, j: (i, 0)),
            const((1, QA_W)),
            const((1, KA_W)),
            const((QA_W, QA_W)),
            const((KA_W, KA_W)),
        ],
        out_specs=(
            tok(N_HEADS_A * LANES),
            tok(KA_W),
            pl.BlockSpec((None, KA_W, tm), lambda i, j: (j, 0, i)),
            tok(QB_W), tok(QB_W), tok(QB_W),
        ),
        out_shape=out_shapes,
        compiler_params=_cparams("arbitrary", "arbitrary"),
        name="inproj",
    )(x, attn_norm[None], w_in_bf, w_in_bf[:, QA_W + KA_W:QA_W + 2 * KA_W].T, cos, sin, qg, kg, _blockdiag(QA_W), _blockdiag(KA_W))


def _gattn_kernel(q_ref, qn_ref, k_ref, kn_ref, vt_ref, o_ref, s0_ref, s1_ref, *, kv_tile):
    tq = q_ref.shape[0]
    group = N_HEADS_A // N_KV_A
    stack_heads = lambda qb: jnp.concatenate([qb[:, LANES * i:LANES * (i + 1)] for i in range(group)], axis=0)
    q4 = stack_heads(q_ref[...])
    n_q = group * tq
    n_kv = k_ref.shape[0] // kv_tile
    ones = jnp.ones((16, kv_tile), BF16)

    def scores(j, dst, keys=k_ref, queries=q4):
        off = pl.multiple_of(j * kv_tile, kv_tile)
        dst[...] = lax.dot_general(keys[pl.ds(off, kv_tile), :], queries, NT_DIMS,
                                   preferred_element_type=F32)

    def consume(j, src, m, acc):
        off = pl.multiple_of(j * kv_tile, kv_tile)
        s = src[...]
        m_new = jnp.maximum(m, jnp.max(s, axis=0, keepdims=True))
        p = jnp.exp2(s - m_new).astype(BF16)
        va = jnp.concatenate([vt_ref[:, pl.ds(off, kv_tile)], ones], axis=0)
        return m_new, acc * jnp.exp2(m - m_new) + jnp.dot(va, p, preferred_element_type=F32)

    def body(i, carry):
        m, acc = carry
        scores(2 * i + 1, s1_ref)
        m, acc = consume(2 * i, s0_ref, m, acc)
        scores(2 * i + 2, s0_ref)
        return consume(2 * i + 1, s1_ref, m, acc)

    @pl.when((pl.program_id(0) == 0) & (pl.program_id(1) == 0) & (pl.program_id(2) == 0))
    def _():
        scores(0, s0_ref)

    init = (jnp.full((1, n_q), NEG, F32), jnp.zeros((HEAD_DIM + 16, n_q), F32))
    m, acc = lax.fori_loop(0, n_kv // 2 - 1, body, init)
    scores(n_kv - 1, s1_ref)
    m, acc = consume(n_kv - 2, s0_ref, m, acc)
    scores(0, s0_ref, kn_ref, stack_heads(qn_ref[...]))
    m, acc = consume(n_kv - 1, s1_ref, m, acc)
    ot = acc[:HEAD_DIM] / acc[HEAD_DIM:HEAD_DIM + 1]
    stacked = jnp.concatenate([ot[:, tq * i:tq * (i + 1)] for i in range(group)], axis=0)
    o_ref[...] = stacked.T.astype(o_ref.dtype)


def _gattn(qa, ka, vt, tq, kv_tile):
    b, s, _ = ka.shape
    group = N_HEADS_A // N_KV_A
    assert s % (2 * kv_tile) == 0 and s >= 2 * kv_tile
    n_i = s // tq
    n_steps = b * N_KV_A * n_i

    def next_step(bi, g, i):
        flat = jnp.minimum((bi * N_KV_A + g) * n_i + i + 1, n_steps - 1)
        return flat // (N_KV_A * n_i), (flat // n_i) % N_KV_A, flat % n_i

    def q_next(bi, g, i):
        nb, ng, ni = next_step(bi, g, i)
        return nb, ni, ng

    return pl.pallas_call(
        functools.partial(_gattn_kernel, kv_tile=kv_tile),
        grid=(b, N_KV_A, n_i),
        in_specs=[
            pl.BlockSpec((None, tq, group * LANES), lambda bi, g, i: (bi, i, g)),
            pl.BlockSpec((None, tq, group * LANES), q_next),
            pl.BlockSpec((None, s, KA_W), lambda bi, g, i: (bi, 0, 0)),
            pl.BlockSpec((None, s, KA_W), lambda bi, g, i: (next_step(bi, g, i)[0], 0, 0)),
            pl.BlockSpec((None, HEAD_DIM, s), lambda bi, g, i: (bi, g, 0)),
        ],
        out_specs=pl.BlockSpec((None, tq, group * HEAD_DIM), lambda bi, g, i: (bi, i, g)),
        out_shape=jax.ShapeDtypeStruct((b, s, QA_W), BF16),
        scratch_shapes=[pltpu.VMEM((kv_tile, group * tq), F32)] * 2,
        compiler_params=_cparams("arbitrary", "arbitrary", "arbitrary"),
        name="gattn",
    )(qa, qa, ka, ka, vt)


NA_STEP_ROWS = 4
NA_WIN_ROWS = NA_STEP_ROWS + NB_ROWS


def _nattn_kernel(q_ref, k_ref, v_ref, tbl_ref, o_ref, *, rows):
    i = pl.program_id(1)
    ws = jnp.clip(NA_STEP_ROWS * i - NB_ROWS // 2, 0, rows - NA_WIN_ROWS)
    off = pl.multiple_of(ws * GRID_W, GRID_W)
    win = NA_WIN_ROWS * GRID_W
    tq = NA_STEP_ROWS * GRID_W
    low = lax.broadcasted_iota(jnp.int32, (tq, LANES), 1) < HEAD_DIM
    for pp in range(N_HEADS_B // 2):
        cols = slice(LANES * pp, LANES * (pp + 1))
        qp = q_ref[:, cols]
        zero = jnp.zeros_like(qp)
        q2 = jnp.concatenate([jnp.where(low, qp, zero), jnp.where(low, zero, qp)], axis=0)
        s = lax.dot_general(q2, k_ref[pl.ds(off, win), cols], NT_DIMS, preferred_element_type=F32)
        s = s + tbl_ref[pp]
        p = jnp.exp2(s - jnp.max(s, axis=-1, keepdims=True))
        l = jnp.sum(p, axis=-1, keepdims=True)
        o2 = jnp.dot(p.astype(BF16), v_ref[pl.ds(off, win), cols], preferred_element_type=F32) / l
        o_ref[:, cols] = jnp.where(low, o2[:tq], o2[tq:]).astype(o_ref.dtype)


def _na_bias_table(rpb):
    qc = jnp.arange(GRID_W)
    kc = jnp.arange(GRID_W)
    cstart = jnp.clip(qc - NB_COLS // 2, 0, GRID_W - NB_COLS)
    valid = (kc[None, :] >= cstart[:, None]) & (kc[None, :] < cstart[:, None] + NB_COLS)
    pad = GRID_W - NB_COLS
    rp = jnp.pad(rpb.astype(F32) * LOG2E, ((0, 0), (0, 0), (pad, pad)))
    toe = jnp.stack([rp[:, :, GRID_W - 1 - q:2 * GRID_W - 1 - q] for q in range(GRID_W)], axis=2)
    toe = jnp.where(valid[None, None], toe, NEG)

    def rows_for(rel, lo):
        d0 = lo - rel + NB_ROWS - 1
        own = toe[:, d0:d0 + NB_ROWS]
        neg = lambda n: jnp.full((N_HEADS_B, n, GRID_W, GRID_W), NEG, F32)
        full = jnp.concatenate([neg(lo), own, neg(NA_WIN_ROWS - NB_ROWS - lo)], axis=1)
        return full.transpose(0, 2, 1, 3).reshape(N_HEADS_B, GRID_W, NA_WIN_ROWS * GRID_W)

    kinds = []
    for rel0, lo_of in ((0, lambda a: 0), (NB_ROWS // 2, lambda a: a), (NB_ROWS, lambda a: NB_ROWS // 2)):
        t = jnp.stack([rows_for(rel0 + a, lo_of(a)) for a in range(NA_STEP_ROWS)], axis=1)
        kinds.append(t.reshape(N_HEADS_B // 2, 2 * NA_STEP_ROWS * GRID_W, NA_WIN_ROWS * GRID_W))
    return jnp.stack(kinds)


def _nattn(qb, kb, vb, rpb):
    b, s, w = qb.shape
    rows = s // GRID_W
    n_steps = rows // NA_STEP_ROWS
    assert rows % NA_STEP_ROWS == 0 and n_steps >= 3 and NA_STEP_ROWS == NB_ROWS // 2
    tbl = _na_bias_table(rpb)
    tq = NA_STEP_ROWS * GRID_W

    def kind_map(bi, i):
        return (jnp.where(i == 0, 0, jnp.where(i == n_steps - 1, 2, 1)), 0, 0, 0)

    return pl.pallas_call(
        functools.partial(_nattn_kernel, rows=rows),
        grid=(b, n_steps),
        in_specs=[
            pl.BlockSpec((None, tq, w), lambda bi, i: (bi, i, 0)),
            pl.BlockSpec((None, s, w), lambda bi, i: (bi, 0, 0)),
            pl.BlockSpec((None, s, w), lambda bi, i: (bi, 0, 0)),
            pl.BlockSpec((None, N_HEADS_B // 2, 2 * tq, NA_WIN_ROWS * GRID_W), kind_map),
        ],
        out_specs=pl.BlockSpec((None, tq, w), lambda bi, i: (bi, i, 0)),
        out_shape=jax.ShapeDtypeStruct((b, s, w), BF16),
        compiler_params=_cparams("arbitrary", "arbitrary"),
        name="nattn",
    )(qb, kb, vb, tbl)


def _memkv_kernel(mem_ref, g_ref, w_ref, k_ref, v_ref):
    m = _rms(mem_ref[...], g_ref[...]).astype(BF16)
    k_ref[...] = jnp.dot(m, w_ref[:, :D_MODEL], preferred_element_type=F32).astype(BF16)
    v_ref[...] = jnp.dot(m, w_ref[:, D_MODEL:], preferred_element_type=F32).astype(BF16)


def _memkv(mem, mem_norm, w_mkv_bf):
    b, m, d = mem.shape
    blk = pl.BlockSpec((None, m, d), lambda i: (i, 0, 0))
    return pl.pallas_call(
        _memkv_kernel,
        grid=(b,),
        in_specs=[blk, pl.BlockSpec((1, d), lambda i: (0, 0)), pl.BlockSpec((d, 2 * d), lambda i: (0, 0))],
        out_specs=(blk, blk),
        out_shape=(jax.ShapeDtypeStruct((b, m, d), BF16),) * 2,
        compiler_params=_cparams("arbitrary"),
        name="memkv",
    )(mem, mem_norm[None], w_mkv_bf)


def _mix_kernel(x_ref, ya_ref, yb_ref, mk_ref, mv_ref, ga_ref, gb_ref, wo_ref, gx_ref, wq_ref, wmo_ref,
                gf_ref, wr_hl_ref, wr_h_ref, x2_ref, h3_ref, aff_ref):
    na = _rms(ya_ref[...].astype(F32), ga_ref[...]).astype(BF16)
    nb = _rms(yb_ref[...].astype(F32), gb_ref[...]).astype(BF16)
    x1 = (x_ref[...]
          + jnp.dot(na, wo_ref[:QA_W, :], preferred_element_type=F32)
          + jnp.dot(nb, wo_ref[QA_W:, :], preferred_element_type=F32))

    h2 = _rms(x1, gx_ref[...]).astype(BF16)
    q = (jnp.dot(h2, wq_ref[...], preferred_element_type=F32) * (MEM_HEAD_DIM ** -0.5 * LOG2E)).astype(BF16)
    heads = []
    for hh in range(MEM_HEADS):
        cols = slice(MEM_HEAD_DIM * hh, MEM_HEAD_DIM * (hh + 1))
        s = lax.dot_general(q[:, cols], mk_ref[:, cols], NT_DIMS, preferred_element_type=F32)
        m = jnp.max(s, axis=-1, keepdims=True)
        p = jnp.exp2(s - m)
        l = jnp.sum(p, axis=-1, keepdims=True)
        heads.append((jnp.dot(p.astype(BF16), mv_ref[:, cols], preferred_element_type=F32) / l).astype(BF16))
    o = jnp.concatenate(heads, axis=1)
    x2 = x1 + jnp.dot(o, wmo_ref[...], preferred_element_type=F32)
    x2_ref[...] = x2

    h3 = _rms(x2, gf_ref[...])
    tm = h3.shape[0]
    for c in range(D_MODEL // LANES):
        h3_ref[pl.ds(c, tm, stride=D_MODEL // LANES), :] = h3[:, LANES * c:LANES * (c + 1)]
    hi, lo = _split_bf16(h3)
    r1 = lax.dot_general(wr_hl_ref[...], hi, NT_DIMS, preferred_element_type=F32)
    r2 = lax.dot_general(wr_h_ref[...], lo, NT_DIMS, preferred_element_type=F32)
    logits = r1[:N_EXPERTS] + r1[N_EXPERTS:] + r2
    e = jnp.exp(logits - jnp.max(logits, axis=0, keepdims=True))
    aff_ref[...] = e / jnp.sum(e, axis=0, keepdims=True)


def _mix(x, ya, yb, mk, mv, out_norm_a, out_norm_b, w_out_bf, xattn_norm, w_mq_bf, w_mo_bf, ffn_norm,
         w_router, tm):
    b, s, d = x.shape
    n = b * s
    tiles_per_seq = s // tm
    wr_t = w_router.T
    wr_hi, wr_lo = _split_bf16(wr_t)
    wr_hl = jnp.concatenate([wr_hi, wr_lo], axis=0)
    const = lambda shape: pl.BlockSpec(shape, lambda i: (0,) * len(shape))
    tok = lambda w: pl.BlockSpec((tm, w), lambda i: (i, 0))
    memb = pl.BlockSpec((None, N_MEM, d), lambda i: (i // tiles_per_seq, 0, 0))
    return pl.pallas_call(
        _mix_kernel,
        grid=(n // tm,),
        in_specs=[
            tok(d), tok(QA_W), tok(QB_W), memb, memb,
            const((1, QA_W)), const((1, QB_W)), const((d, d)),
            const((1, d)), const((d, d)), const((d, d)),
            const((1, d)), const((2 * N_EXPERTS, d)), const((N_EXPERTS, d)),
        ],
        out_specs=(tok(d), pl.BlockSpec((tm * (d // LANES), LANES), lambda i: (i, 0)),
                   pl.BlockSpec((N_EXPERTS, tm), lambda i: (0, i))),
        out_shape=(
            jax.ShapeDtypeStruct((n, d), F32),
            jax.ShapeDtypeStruct((n * (d // LANES), LANES), F32),
            jax.ShapeDtypeStruct((N_EXPERTS, n), F32),
        ),
        compiler_params=_cparams("arbitrary"),
        name="mix",
    )(x.reshape(n, d), ya.reshape(n, QA_W), yb.reshape(n, QB_W), mk, mv, out_norm_a[None], out_norm_b[None],
      w_out_bf, xattn_norm[None], w_mq_bf, w_mo_bf, ffn_norm[None], wr_hl, wr_hi)


def _front(x, mem, p, *, tm_in=1024, tq=128, kv_tile=1024, tm_mix=512):
    qa, ka, vt, qb, kb, vb = _inproj(x, p["attn_norm"], p["w_in"], p["q_norm"], p["k_norm"], tm_in)
    ya = _gattn(qa, ka, vt, tq, min(kv_tile, x.shape[1] // 2))
    yb = _nattn(qb, kb, vb, p["rpb"])
    mk, mv = _memkv(mem, p["mem_norm"], p["w_mkv"])
    return _mix(x, ya, yb, mk, mv, p["out_norm_a"], p["out_norm_b"], p["w_out"], p["xattn_norm"],
                p["w_mq"], p["w_mo"], p["ffn_norm"], p["w_router"], tm_mix)


def _tri(n, strict_lower=False):
    i = lax.broadcasted_iota(jnp.int32, (n, n), 0)
    j = lax.broadcasted_iota(jnp.int32, (n, n), 1)
    return ((j < i) if strict_lower else (i <= j)).astype(BF16)


def _select_kernel(aff_ref, slot_ref, sel_ref, off_ref, *, cap):
    n_e, rows, _ = aff_ref.shape

    def bits_of(e):
        return lax.bitcast_convert_type(aff_ref[e], jnp.int32)

    def step(i, prefixes):
        bit = jnp.left_shift(jnp.int32(1), 30 - i)
        out = []
        for e in range(n_e):
            cand = prefixes[e] | bit
            cnt = jnp.sum(jnp.where(bits_of(e) >= cand, 1.0, 0.0), axis=(0, 1), keepdims=True)
            out.append(jnp.where(cnt >= float(cap), cand, prefixes[e]))
        return tuple(out)

    thr = lax.fori_loop(0, 31, step, tuple(jnp.zeros((1, 1), jnp.int32) for _ in range(n_e)))

    u_lane = _tri(LANES)
    u_rows = _tri(rows)
    l_rows = _tri(rows, strict_lower=True)
    ones_l = jnp.ones((LANES, LANES), BF16)
    ones_8 = jnp.ones((8, LANES), BF16)

    def prefix(mask_bf):
        within = jnp.dot(mask_bf, u_lane, preferred_element_type=F32)
        row_tot = jnp.dot(mask_bf, ones_l, preferred_element_type=F32)
        before = jnp.dot(l_rows, row_tot.astype(BF16), preferred_element_type=F32)
        return within, before

    for e in range(n_e):
        b = bits_of(e)
        gt = b > thr[e]
        eq = b == thr[e]
        need = float(cap) - jnp.sum(jnp.where(gt, 1.0, 0.0), axis=(0, 1), keepdims=True)
        eq_f = jnp.where(eq, 1.0, 0.0)
        within, before = prefix(eq_f.astype(BF16))
        sel = gt | (eq & ((before + within - eq_f) < need))
        sel_f = jnp.where(sel, 1.0, 0.0)
        sel_bf = sel_f.astype(BF16)
        within, before = prefix(sel_bf)
        rank = (before + within - 1.0).astype(jnp.int32)
        slot_ref[e] = jnp.where(sel, rank + e * cap, -1)
        sel_ref[e] = sel_bf
        cnt_row = lax.dot_general(ones_8, sel_bf, NT_DIMS, preferred_element_type=F32)
        end_row = jnp.dot(cnt_row.astype(BF16), u_rows, preferred_element_type=F32)
        off_ref[e] = end_row - cnt_row


def _select(aff3, cap):
    n_e, rows, _ = aff3.shape
    return pl.pallas_call(
        functools.partial(_select_kernel, cap=cap),
        out_shape=(
            jax.ShapeDtypeStruct((n_e, rows, LANES), jnp.int32),
            jax.ShapeDtypeStruct((n_e, rows, LANES), BF16),
            jax.ShapeDtypeStruct((n_e, 8, rows), F32),
        ),
        compiler_params=pltpu.CompilerParams(vmem_limit_bytes=VMEM_LIMIT),
        name="select",
    )(aff3)


def _compact_kernel(sel_ref, aff_ref, idx_ref, gate_ref, *, jt):
    rows = sel_ref.shape[0]
    sel = sel_ref[...]
    ones_8 = jnp.ones((8, LANES), BF16)
    cnt_row = lax.dot_general(ones_8, sel, NT_DIMS, preferred_element_type=F32)
    end_row = jnp.dot(cnt_row.astype(BF16), _tri(rows), preferred_element_type=F32)
    row_end = end_row[0:1, :]
    row_off = row_end - cnt_row[0:1, :]

    j0 = pl.program_id(1) * jt
    j_r = (j0 + lax.broadcasted_iota(jnp.int32, (jt, rows), 0)).astype(F32)
    below = jnp.where(row_end <= j_r, 1.0, 0.0).astype(BF16)
    hit = jnp.where((row_end > j_r) & (row_off <= j_r), 1.0, 0.0).astype(BF16)

    a = aff_ref[...]
    a0 = a.astype(BF16)
    a1 = (a - a0.astype(F32)).astype(BF16)
    a2 = (a - a0.astype(F32) - a1.astype(F32)).astype(BF16)
    picked = jnp.dot(hit, jnp.concatenate([sel, a0, a1, a2], axis=1), preferred_element_type=F32)

    cnt_b = jnp.dot(sel, jnp.ones((LANES, LANES), BF16), preferred_element_type=F32)
    off_b = jnp.dot(below, cnt_b.astype(BF16), preferred_element_type=F32)
    sel_row = picked[:, :LANES]
    within = jnp.dot(sel_row.astype(BF16), _tri(LANES), preferred_element_type=F32)
    j_l = (j0 + lax.broadcasted_iota(jnp.int32, (jt, LANES), 0)).astype(F32)
    pick = (sel_row > 0.5) & (within == (j_l - off_b + 1.0))
    pick_bf = jnp.where(pick, 1.0, 0.0).astype(BF16)

    sub = lax.broadcasted_iota(jnp.int32, (8, LANES), 0)
    lane_id = jnp.where(sub == 0, lax.broadcasted_iota(jnp.int32, (8, LANES), 1), 0).astype(BF16)
    lane_pos = lax.dot_general(lane_id, pick_bf, NT_DIMS, preferred_element_type=F32)
    sub_r = lax.broadcasted_iota(jnp.int32, (8, rows), 0)
    r_id = lax.broadcasted_iota(jnp.int32, (8, rows), 1)
    r_split = jnp.where(sub_r == 0, r_id // 16, jnp.where(sub_r == 1, r_id % 16, 0)).astype(BF16)
    row_pos = lax.dot_general(r_split, hit, NT_DIMS, preferred_element_type=F32)
    tok = (row_pos[0:1, :] * 16.0 + row_pos[1:2, :]) * float(LANES) + lane_pos[0:1, :]
    idx_ref[...] = tok.astype(jnp.int32)

    gate = jnp.zeros((1, jt), F32)
    for i in range(1, 4):
        g = jnp.where(pick, picked[:, LANES * i:LANES * (i + 1)], 0.0).astype(BF16)
        gate = gate + lax.dot_general(ones_8, g, NT_DIMS, preferred_element_type=F32)[0:1, :]
    gate_ref[...] = gate


def _compact(sel, aff3, cap, jt):
    n_e, rows, _ = sel.shape
    blk = pl.BlockSpec((None, rows, LANES), lambda e, j: (e, 0, 0))
    out_blk = pl.BlockSpec((None, 1, jt), lambda e, j: (e, 0, j))
    return pl.pallas_call(
        functools.partial(_compact_kernel, jt=jt),
        grid=(n_e, cap // jt),
        in_specs=[blk, blk],
        out_specs=(out_blk, out_blk),
        out_shape=(jax.ShapeDtypeStruct((n_e, 1, cap), jnp.int32), jax.ShapeDtypeStruct((n_e, 1, cap), F32)),
        compiler_params=_cparams("arbitrary", "arbitrary"),
        name="compact",
    )(sel, aff3)


def _ffn_kernel(idx_ref, nxt_ref, gate_ref, h_hbm, wg_ref, wu_ref, wd_ref, o_ref, xbuf, sems, *, fc):
    tile = D_MODEL // LANES
    ts = xbuf.shape[1] // tile
    g = pl.program_id(0)
    cur = g % 2

    def row_copy(half, j, token):
        return pltpu.make_async_copy(h_hbm.at[pl.ds(pl.multiple_of(token * tile, tile), tile), :],
                                     xbuf.at[half, pl.ds(j * tile, tile), :], sems.at[half])

    def gather(ids_ref, half):
        for j in range(ts):
            row_copy(half, j, ids_ref[0, j]).start()

    @pl.when(g == 0)
    def _():
        gather(idx_ref, 0)

    has_next = g + 1 < pl.num_programs(0)
    for half in range(2):
        @pl.when(has_next & (cur != half))
        def _(half=half):
            gather(nxt_ref, half)

    def drain(j, c):
        row_copy(cur, j, 0).wait()
        return c

    lax.fori_loop(0, ts, drain, 0, unroll=8)

    xe = jnp.concatenate([xbuf[cur, pl.ds(c, ts, stride=tile), :] for c in range(tile)], axis=1).astype(BF16)
    acc = jnp.zeros((ts, D_MODEL), F32)
    for c in range(D_FF // fc):
        cols = slice(fc * c, fc * (c + 1))
        g = jnp.dot(xe, wg_ref[:, cols], preferred_element_type=F32)
        u = jnp.dot(xe, wu_ref[:, cols], preferred_element_type=F32)
        he = (g / (1.0 + jnp.exp(-g)) * u).astype(BF16)
        acc = acc + jnp.dot(he, wd_ref[cols, :], preferred_element_type=F32)
    gate_col = jnp.broadcast_to(gate_ref[...], (LANES, ts)).T[:, 0:1]
    o_ref[...] = (acc * gate_col).astype(o_ref.dtype)


def _ffn(idx, gate, h3, wg, wu, wd, ts, fc):
    n_e, _, cap = idx.shape
    d = D_MODEL
    nt = cap // ts
    n_steps = n_e * nt
    idx3 = idx.reshape(n_steps, 1, ts)
    ids = lambda shift: pl.BlockSpec((None, 1, ts), lambda g: (jnp.minimum(g + shift, n_steps - 1), 0, 0),
                                     memory_space=pltpu.SMEM)
    wspec = lambda r, c: pl.BlockSpec((None, r, c), lambda g: (g // nt, 0, 0))
    return pl.pallas_call(
        functools.partial(_ffn_kernel, fc=fc),
        grid=(n_steps,),
        in_specs=[
            ids(0), ids(1),
            pl.BlockSpec((None, 1, ts), lambda g: (g // nt, 0, g % nt)),
            pl.BlockSpec(memory_space=pl.ANY),
            wspec(d, D_FF), wspec(d, D_FF), wspec(D_FF, d),
        ],
        out_specs=pl.BlockSpec((ts, d), lambda g: (g, 0)),
        out_shape=jax.ShapeDtypeStruct((n_e * cap, d), BF16),
        scratch_shapes=[pltpu.VMEM((2, ts * (d // LANES), LANES), F32), pltpu.SemaphoreType.DMA((2,))],
        compiler_params=_cparams("arbitrary"),
        name="ffn",
    )(idx3, idx3, gate, h3, wg, wu, wd)


COMBINE_WIN = 64


def _combine_kernel(base_ref, rounds_ref, x_ref, slot_ref, ye_hbm, g_ref, o_ref, buf, sems):
    n_e, t = slot_ref.shape
    total = ye_hbm.shape[0]
    tt = pl.program_id(0)
    cur = tt % 2
    w_iota = lax.broadcasted_iota(jnp.int32, (COMBINE_WIN, t), 0)

    def windows(tile, r):
        lows = [base_ref[e, tile] + r * COMBINE_WIN for e in range(n_e)]
        return lows, [pl.multiple_of(jnp.minimum(lo, total - COMBINE_WIN), 16) for lo in lows]

    def win_copy(half, e, start):
        return pltpu.make_async_copy(ye_hbm.at[pl.ds(start, COMBINE_WIN), :],
                                     buf.at[half, pl.ds(e * COMBINE_WIN, COMBINE_WIN), :], sems.at[half])

    def fetch(tile, r, half):
        _, starts = windows(tile, r)
        for e in range(n_e):
            win_copy(half, e, starts[e]).start()

    def scatter_add(r, acc):
        lows, starts = windows(tt, r)
        for e in range(n_e):
            win_copy(cur, e, starts[e]).wait()
        onehot_t = jnp.concatenate(
            [jnp.where((slot_ref[e:e + 1, :] == starts[e] + w_iota) & (slot_ref[e:e + 1, :] >= lows[e]),
                       1.0, 0.0).astype(BF16) for e in range(n_e)],
            axis=0)
        return acc + lax.dot_general(onehot_t, buf[cur], TN_DIMS, preferred_element_type=F32)

    @pl.when(tt == 0)
    def _():
        fetch(tt, 0, cur)

    @pl.when(tt + 1 < pl.num_programs(0))
    def _():
        fetch(tt + 1, 0, 1 - cur)

    y = scatter_add(0, jnp.zeros((t, D_MODEL), F32))

    def later_round(r, acc):
        fetch(tt, r, cur)
        return scatter_add(r, acc)

    y = lax.fori_loop(1, rounds_ref[tt], later_round, y)
    o_ref[...] = _rms(x_ref[...] + y, g_ref[...])


def _combine(x2, slot, ye, base, rounds, final_norm, t):
    n, d = x2.shape
    n_e = slot.shape[0]
    return pl.pallas_call(
        _combine_kernel,
        grid_spec=pltpu.PrefetchScalarGridSpec(
            num_scalar_prefetch=2,
            grid=(n // t,),
            in_specs=[
                pl.BlockSpec((t, d), lambda i, *_: (i, 0)),
                pl.BlockSpec((n_e, t), lambda i, *_: (0, i)),
                pl.BlockSpec(memory_space=pl.ANY),
                pl.BlockSpec((1, d), lambda i, *_: (0, 0)),
            ],
            out_specs=pl.BlockSpec((t, d), lambda i, *_: (i, 0)),
            scratch_shapes=[pltpu.VMEM((2, n_e * COMBINE_WIN, d), BF16), pltpu.SemaphoreType.DMA((2,))],
        ),
        out_shape=jax.ShapeDtypeStruct((n, d), F32),
        compiler_params=_cparams("arbitrary"),
        name="combine",
    )(base, rounds, x2, slot, ye, final_norm[None])


def _moe(x2, h3, aff, p, *, jt=512, ts=512, fc=512, t=256):
    n, d = x2.shape
    n_e = aff.shape[0]
    cap = EC_FACTOR * n // n_e
    rows = n // LANES
    aff3 = aff.reshape(n_e, rows, LANES)
    slot, sel, off = _select(aff3, cap)
    idx, gate = _compact(sel, aff3, cap, min(jt, cap))
    ye = _ffn(idx, gate, h3, p["w_gate"], p["w_up"], p["w_down"], min(ts, cap), fc)

    rows_per_tile = t // LANES
    first = off[:, 0, ::rows_per_tile].astype(jnp.int32) + cap * jnp.arange(n_e, dtype=jnp.int32)[:, None]
    nxt = jnp.concatenate([first[:, 1:], (cap * (jnp.arange(n_e, dtype=jnp.int32) + 1))[:, None]], axis=1)
    base = (first // 16) * 16
    rounds = jnp.max((nxt - base + COMBINE_WIN - 1) // COMBINE_WIN, axis=0).astype(jnp.int32)
    rounds = jnp.where(jnp.max(nxt - first, axis=0) > 0, rounds, 0)
    return _combine(x2, slot.reshape(n_e, n), ye, base, rounds, p["final_norm"], t)


def kernel(x_prompt, x_sample, mem_prompt, mem_sample, attn_norm, w_in, q_norm, k_norm, rpb, out_norm_a,
           out_norm_b, w_out, xattn_norm, mem_norm, w_mq, w_mkv, w_mo, ffn_norm, w_router, w_gate, w_up,
           w_down, final_norm):
    assert attn_norm.shape[0] == 1, "single-layer encoder"
    p = dict(attn_norm=attn_norm[0], w_in=w_in[0].astype(BF16), q_norm=q_norm[0], k_norm=k_norm[0],
             rpb=rpb[0], out_norm_a=out_norm_a[0], out_norm_b=out_norm_b[0], w_out=w_out[0].astype(BF16),
             xattn_norm=xattn_norm[0], mem_norm=mem_norm[0], w_mq=w_mq[0].astype(BF16),
             w_mkv=w_mkv[0].astype(BF16), w_mo=w_mo[0].astype(BF16), ffn_norm=ffn_norm[0],
             w_router=w_router[0], w_gate=w_gate[0].astype(BF16), w_up=w_up[0].astype(BF16),
             w_down=w_down[0].astype(BF16), final_norm=final_norm)
    outs = []
    for x, mem in ((x_prompt, mem_prompt), (x_sample, mem_sample)):
        x2, h3, aff = _front(x, mem, p)
        outs.append(_moe(x2, h3, aff, p).reshape(x.shape))
    return tuple(outs)
```
